```python
import jax
import jax.numpy as jnp
from jax import lax
import numpy as np

D_MODEL = 2048
BATCH = 2
SEQ = 16384
DEPTH = 2

GRID_W = 64
CTX_LEN = 256
HEAD_DIM = 64
EPS = 1e-6
GN_EPS = 64e-5
NEG_INF = -1e30

ATT_HEADS = D_MODEL // 128
ATT_KV_HEADS = ATT_HEADS // 4
ATT_GROUP = ATT_HEADS // ATT_KV_HEADS
ATT_WIDTH = ATT_HEADS * HEAD_DIM
KV_WIDTH = ATT_KV_HEADS * HEAD_DIM
WINDOW = 128
ATT_BLOCK = 128
ROPE_THETA = 10000.0
ROPE_AXIS_DIM = HEAD_DIM // 2

RWKV_HEADS = D_MODEL // 256
RWKV_WIDTH = RWKV_HEADS * HEAD_DIM
W_LORA = 64
A_LORA = 64
G_LORA = 128
N_DIR = 2

CONV_WIDTH = D_MODEL // 4
CONV_KSIZE = 31

MIX_WIDTH = ATT_WIDTH + RWKV_WIDTH + CONV_WIDTH
ATT_COLS = ATT_WIDTH + 2 * KV_WIDTH
RWKV_COLS = 3 * RWKV_WIDTH + N_DIR * W_LORA + N_DIR * A_LORA + G_LORA
CONV_COLS = 2 * CONV_WIDTH
IN_COLS = ATT_COLS + RWKV_COLS + CONV_COLS

PEER_HEADS = 8
PEER_N_KEYS = 128
PEER_N_EXPERTS = PEER_N_KEYS ** 2
PEER_KEY_DIM = 256
PEER_TOPK = 16
PEER_TOKEN_BLOCK = 128

kernel_name = 'hybrid_dit_rwkv7_conformer_swa_peer'


def rms_norm(x, g):
    xf = x.astype(jnp.float32)
    y = xf * lax.rsqrt(jnp.mean(xf * xf, axis=-1, keepdims=True) + EPS)
    return (y * g.astype(jnp.float32)).astype(x.dtype)


def modulate(x, g, shift, scale):
    return rms_norm(x, g) * (1.0 + scale) + shift


def split_cols(p, sizes):
    offs = [int(o) for o in np.cumsum(sizes)[:-1]]
    return jnp.split(p, offs, axis=-1)


def axial_rope(x, rows, cols):
    inv = ROPE_THETA ** (-jnp.arange(0, ROPE_AXIS_DIM, 2, dtype=jnp.float32) / ROPE_AXIS_DIM)

    def rotate(xp, pos):
        ang = pos.astype(jnp.float32)[:, None] * inv[None, :]
        cos = jnp.cos(ang)[None, :, None, :]
        sin = jnp.sin(ang)[None, :, None, :]
        x1, x2 = jnp.split(xp.astype(jnp.float32), 2, axis=-1)
        return jnp.concatenate([x1 * cos - x2 * sin, x1 * sin + x2 * cos], axis=-1)

    out = jnp.concatenate([rotate(x[..., :ROPE_AXIS_DIM], rows),
                           rotate(x[..., ROPE_AXIS_DIM:], cols)], axis=-1)
    return out.astype(x.dtype)


def attn_q(p, q_g):
    B, L = p.shape[:2]
    return rms_norm(p[..., :ATT_WIDTH].reshape(B, L, ATT_HEADS, HEAD_DIM), q_g)


def attn_kv(p, k_g):
    B, L = p.shape[:2]
    k = p[..., ATT_WIDTH:ATT_WIDTH + KV_WIDTH].reshape(B, L, ATT_KV_HEADS, HEAD_DIM)
    v = p[..., ATT_WIDTH + KV_WIDTH:ATT_COLS].reshape(B, L, ATT_KV_HEADS, HEAD_DIM)
    return rms_norm(k, k_g), v


def window_attention(q, k, v, kc, vc, sink):
    B, L = q.shape[:2]
    C = kc.shape[1]
    nb = L // ATT_BLOCK
    scale = HEAD_DIM ** -0.5
    qb = jnp.moveaxis(q.reshape(B, nb, ATT_BLOCK, ATT_KV_HEADS, ATT_GROUP, HEAD_DIM), 1, 0)
    pad = ((0, 0), (ATT_BLOCK, ATT_BLOCK), (0, 0), (0, 0))
    kp = jnp.pad(k, pad)
    vp = jnp.pad(v, pad)
    qi = jnp.arange(ATT_BLOCK)[:, None]
    si = jnp.arange(3 * ATT_BLOCK)[None, :]
    band = jnp.abs(si - ATT_BLOCK - qi) <= WINDOW
    sink_logit = jnp.broadcast_to(
        sink.astype(jnp.float32).reshape(ATT_KV_HEADS, ATT_GROUP)[None, :, :, None, None],
        (B, ATT_KV_HEADS, ATT_GROUP, ATT_BLOCK, 1))

    def block(args):
        n, qn = args
        kw = lax.dynamic_slice_in_dim(kp, n * ATT_BLOCK, 3 * ATT_BLOCK, axis=1)
        vw = lax.dynamic_slice_in_dim(vp, n * ATT_BLOCK, 3 * ATT_BLOCK, axis=1)
        kpos = (n - 1) * ATT_BLOCK + jnp.arange(3 * ATT_BLOCK)
        mask = band & ((kpos >= 0) & (kpos < L))[None, :]
        s_win = jnp.einsum('bqkgd,bskd->bkgqs', qn, kw).astype(jnp.float32) * scale
        s_win = jnp.where(mask, s_win, NEG_INF)
        s_ctx = jnp.einsum('bqkgd,bckd->bkgqc', qn, kc).astype(jnp.float32) * scale
        probs = jax.nn.softmax(jnp.concatenate([s_win, s_ctx, sink_logit], axis=-1), axis=-1)
        p_win = probs[..., :3 * ATT_BLOCK].astype(v.dtype)
        p_ctx = probs[..., 3 * ATT_BLOCK:3 * ATT_BLOCK + C].astype(v.dtype)
        return (jnp.einsum('bkgqs,bskd->bqkgd', p_win, vw)
                + jnp.einsum('bkgqc,bckd->bqkgd', p_ctx, vc))

    out = lax.map(block, (jnp.arange(nb), qb))
    return jnp.moveaxis(out, 0, 1).reshape(B, L, ATT_WIDTH)


def ctx_attention(qc, kc, vc, sink):
    B, C = qc.shape[:2]
    qg = qc.reshape(B, C, ATT_KV_HEADS, ATT_GROUP, HEAD_DIM)
    s = jnp.einsum('bqkgd,bckd->bkgqc', qg, kc).astype(jnp.float32) * HEAD_DIM ** -0.5
    snk = jnp.broadcast_to(
        sink.astype(jnp.float32).reshape(ATT_KV_HEADS, ATT_GROUP)[None, :, :, None, None],
        (B, ATT_KV_HEADS, ATT_GROUP, C, 1))
    probs = jax.nn.softmax(jnp.concatenate([s, snk], axis=-1), axis=-1)[..., :C]
    out = jnp.einsum('bkgqc,bckd->bqkgd', probs.astype(vc.dtype), vc)
    return out.reshape(B, C, ATT_WIDTH)


def centred_shift(p, mu):
    zero = jnp.zeros_like(p[:, :1])
    prev = jnp.concatenate([zero, p[:, :-1]], axis=1)
    nxt = jnp.concatenate([p[:, 1:], zero], axis=1)
    return p + mu * (0.5 * (prev + nxt) - p)


def rwkv_features(p, mu, w0, w_up, a0, a_up, g_up, k_k, k_a):
    B, L, _ = p.shape
    p = centred_shift(p, mu).astype(jnp.float32)
    r, k, v, wd, ad, gd = split_cols(
        p, [RWKV_WIDTH, RWKV_WIDTH, RWKV_WIDTH, N_DIR * W_LORA, N_DIR * A_LORA, G_LORA])
    wd = wd.reshape(B, L, N_DIR, W_LORA)
    ad = ad.reshape(B, L, N_DIR, A_LORA)
    w = -jax.nn.softplus(-(w0 + jnp.einsum('bldr,drc->bldc', jnp.tanh(wd), w_up))) - 0.5
    decay = jnp.exp(-jnp.exp(w))
    a = jax.nn.sigmoid(a0 + jnp.einsum('bldr,drc->bldc', ad, a_up))
    g = jax.nn.sigmoid(gd) @ g_up

    def heads(t):
        return t.reshape(B, L, RWKV_HEADS, HEAD_DIM)

    kk = heads(k * k_k)
    kk = kk * lax.rsqrt(jnp.sum(kk * kk, axis=-1, keepdims=True) + 1e-12)
    k_dir = k[:, :, None, :] * (1.0 + (a - 1.0) * k_a)
    dirs = [(heads(decay[:, :, d]), heads(k_dir[:, :, d]), heads(a[:, :, d])) for d in range(N_DIR)]
    return heads(r), heads(k), heads(v), kk, g, dirs


def rwkv_scan(r, w, k, v, kk, a, s0, reverse, emit):
    xs = tuple(jnp.moveaxis(t, 1, 0) for t in (r, w, k, v, kk, a))

    def step(S, inp):
        r_t, w_t, k_t, v_t, kk_t, a_t = inp
        sa = jnp.einsum('bhij,bhj->bhi', S, -kk_t)
        S = (S * w_t[:, :, None, :] + sa[..., None] * (kk_t * a_t)[:, :, None, :]
             + v_t[..., None] * k_t[:, :, None, :])
        y = jnp.einsum('bhij,bhj->bhi', S, r_t) if emit else None
        return S, y

    S, ys = lax.scan(step, s0, xs, reverse=reverse)
    return S, (jnp.moveaxis(ys, 0, 1) if emit else None)


def rwkv_output(y, r, k, v, g, r_k, ln_g, ln_b):
    B, L = y.shape[:2]
    mean = jnp.mean(y, axis=-1, keepdims=True)
    var = jnp.mean(jnp.square(y - mean), axis=-1, keepdims=True)
    yn = (y - mean) * lax.rsqrt(var + GN_EPS)
    yn = yn * ln_g.reshape(RWKV_HEADS, HEAD_DIM) + ln_b.reshape(RWKV_HEADS, HEAD_DIM)
    bonus = jnp.sum(r * k * r_k, axis=-1, keepdims=True) * v
    return (yn + bonus).reshape(B, L, RWKV_WIDTH) * g


def rwkv_mixer(p_lat, p_ctx, emit_ctx, mu, w0, w_up, a0, a_up, g_up, k_k, k_a, r_k, ln_g, ln_b):
    r, kb, v, kk, g, dirs = rwkv_features(p_lat, mu, w0, w_up, a0, a_up, g_up, k_k, k_a)
    rc, kbc, vc, kkc, gc, dirsc = rwkv_features(p_ctx, mu, w0, w_up, a0, a_up, g_up, k_k, k_a)
    B = p_lat.shape[0]
    zero = jnp.zeros((B, RWKV_HEADS, HEAD_DIM, HEAD_DIM), jnp.float32)
    y_lat, y_ctx = [], []
    for d in range(N_DIR):
        dec, kd, a = dirs[d]
        decc, kdc, ac = dirsc[d]
        s_ctx, yc = rwkv_scan(rc, decc, kdc, vc, kkc, ac, zero, reverse=(d == 1), emit=emit_ctx)
        _, yl = rwkv_scan(r, dec, kd, v, kk, a, s_ctx, reverse=(d == 1), emit=True)
        y_lat.append(yl)
        y_ctx.append(yc)
    out_lat = rwkv_output(y_lat[0] + y_lat[1], r, kb, v, g, r_k, ln_g, ln_b)
    if not emit_ctx:
        return out_lat, None
    out_ctx = rwkv_output(y_ctx[0] + y_ctx[1], rc, kbc, vc, gc, r_k, ln_g, ln_b)
    return out_lat, out_ctx


def conv_module(p, w_dw, b_dw, norm_g):
    a, b = jnp.split(p, 2, axis=-1)
    u = a * jax.nn.sigmoid(b)
    u = lax.conv_general_dilated(
        u, w_dw[:, None, :].astype(u.dtype), window_strides=(1,),
        padding=[((CONV_KSIZE - 1) // 2, (CONV_KSIZE - 1) // 2)],
        dimension_numbers=('NWC', 'WIO', 'NWC'), feature_group_count=CONV_WIDTH) + b_dw
    return jax.nn.silu(rms_norm(u, norm_g))


def peer_ffn(h, wq, sub_keys, u_tab, v_tab):
    B, L, D = h.shape
    q = (h @ wq).reshape(B, L, PEER_HEADS, 2, PEER_KEY_DIM // 2)
    s = jnp.einsum('blhpd,hpkd->blhpk', q, sub_keys).astype(jnp.float32)
    sv, si = lax.top_k(s, PEER_TOPK)
    cand = sv[..., 0, :, None] + sv[..., 1, None, :]
    cv, ci = lax.top_k(cand.reshape(B, L, PEER_HEADS, PEER_TOPK * PEER_TOPK), PEER_TOPK)
    i1 = jnp.take_along_axis(si[..., 0, :], ci // PEER_TOPK, axis=-1)
    i2 = jnp.take_along_axis(si[..., 1, :], ci % PEER_TOPK, axis=-1)
    expert = i1 * PEER_N_KEYS + i2
    gate = jax.nn.softmax(cv, axis=-1).astype(h.dtype)
    n_blk = (B * L) // PEER_TOKEN_BLOCK
    n_sel = PEER_HEADS * PEER_TOPK
    xt = h.reshape(n_blk, PEER_TOKEN_BLOCK, D)
    it = expert.reshape(n_blk, PEER_TOKEN_BLOCK, n_sel)
    gt = gate.reshape(n_blk, PEER_TOKEN_BLOCK, n_sel)

    def block(args):
        xb, ib, gb = args
        u = jnp.take(u_tab, ib, axis=0)
        vv = jnp.take(v_tab, ib, axis=0)
        act = jax.nn.gelu(jnp.einsum('td,ted->te', xb, u), approximate=False)
        return jnp.einsum('te,ted->td', gb * act, vv)

    return lax.map(block, (xt, it, gt)).reshape(B, L, D)


def setup_inputs(seed: int = 0) -> dict:
    key = jax.random.key(seed)
    ks = iter(jax.random.split(key, 40))
    D = D_MODEL

    def nrm(shape, s):
        return jax.random.normal(next(ks), shape, jnp.float32) * s

    return {
        'x': nrm((BATCH, SEQ, D), 1.0),
        'c': nrm((BATCH, D), 1.0),
        'ctx': nrm((BATCH, CTX_LEN, D), 1.0),
        'c_ctx': nrm((D,), 1.0),
        'norm1_g': 1.0 + nrm((DEPTH, D), 0.02),
        'norm2_g': 1.0 + nrm((DEPTH, D), 0.02),
        'w_ada': nrm((DEPTH, D, 6 * D), 0.5 * D ** -0.5),
        'b_ada': nrm((DEPTH, 6 * D), 0.02),
        'w_in': nrm((DEPTH, D, IN_COLS), D ** -0.5),
        'w_out': nrm((DEPTH, MIX_WIDTH, D), MIX_WIDTH ** -0.5),
        'q_norm_g': 1.0 + nrm((DEPTH, HEAD_DIM), 0.02),
        'k_norm_g': 1.0 + nrm((DEPTH, HEAD_DIM), 0.02),
        'attn_sink': nrm((DEPTH, ATT_HEADS), 0.5),
        'rwkv_mu': jax.random.uniform(next(ks), (DEPTH, RWKV_COLS), jnp.float32),
        'rwkv_w0': -1.0 + nrm((DEPTH, N_DIR, RWKV_WIDTH), 0.5),
        'rwkv_w_up': nrm((DEPTH, N_DIR, W_LORA, RWKV_WIDTH), 0.1),
        'rwkv_a0': nrm((DEPTH, N_DIR, RWKV_WIDTH), 0.5),
        'rwkv_a_up': nrm((DEPTH, N_DIR, A_LORA, RWKV_WIDTH), 0.1),
        'rwkv_g_up': nrm((DEPTH, G_LORA, RWKV_WIDTH), G_LORA ** -0.5),
        'rwkv_k_k': 1.0 + nrm((DEPTH, RWKV_WIDTH), 0.1),
        'rwkv_k_a': 1.0 + nrm((DEPTH, RWKV_WIDTH), 0.1),
        'rwkv_r_k': nrm((DEPTH, RWKV_HEADS, HEAD_DIM), 0.1),
        'rwkv_ln_g': 1.0 + nrm((DEPTH, RWKV_WIDTH), 0.02),
        'rwkv_ln_b': nrm((DEPTH, RWKV_WIDTH), 0.02),
        'conv_w': nrm((DEPTH, CONV_KSIZE, CONV_WIDTH), CONV_KSIZE ** -0.5),
        'conv_b': nrm((DEPTH, CONV_WIDTH), 0.02),
        'conv_norm_g': 1.0 + nrm((DEPTH, CONV_WIDTH), 0.02),
        'peer_wq': nrm((DEPTH, D, PEER_HEADS * PEER_KEY_DIM), D ** -0.5),
        'peer_keys': nrm((DEPTH, PEER_HEADS, 2, PEER_N_KEYS, PEER_KEY_DIM // 2), (PEER_KEY_DIM // 2) ** -0.5),
        'peer_u': nrm((DEPTH, PEER_N_EXPERTS, D), D ** -0.5),
        'peer_v': nrm((DEPTH, PEER_N_EXPERTS, D), PEER_HEADS ** -0.5),
    }


def reference(x, c, ctx, c_ctx, norm1_g, norm2_g, w_ada, b_ada, w_in, w_out,
              q_norm_g, k_norm_g, attn_sink, rwkv_mu, rwkv_w0, rwkv_w_up, rwkv_a0,
              rwkv_a_up, rwkv_g_up, rwkv_k_k, rwkv_k_a, rwkv_r_k, rwkv_ln_g, rwkv_ln_b,
              conv_w, conv_b, conv_norm_g, peer_wq, peer_keys, peer_u, peer_v):
    B, L, _ = x.shape
    ROWS = L // GRID_W
    rows = jnp.repeat(jnp.arange(ROWS, dtype=jnp.int32), GRID_W)
    cols = jnp.tile(jnp.arange(GRID_W, dtype=jnp.int32), ROWS)
    xc = ctx
    for i in range(DEPTH):
        last = i == DEPTH - 1
        mod = jax.nn.silu(c) @ w_ada[i] + b_ada[i]
        sh1, sc1, g1, sh2, sc2, g2 = jnp.split(mod[:, None, :], 6, axis=-1)
        modc = jax.nn.silu(c_ctx) @ w_ada[i] + b_ada[i]
        csh1, csc1, cg1, csh2, csc2, cg2 = jnp.split(modc[None, None, :], 6, axis=-1)

        h = modulate(x, norm1_g[i], sh1, sc1)
        hc = modulate(xc, norm1_g[i], csh1, csc1)
        p = h @ w_in[i]
        pc = hc @ (w_in[i][:, :ATT_COLS + RWKV_COLS] if last else w_in[i])
        p_att, p_rw, p_cv = split_cols(p, [ATT_COLS, RWKV_COLS, CONV_COLS])
        pc_att = pc[..., :ATT_COLS]
        pc_rw = pc[..., ATT_COLS:ATT_COLS + RWKV_COLS]

        q = axial_rope(attn_q(p_att, q_norm_g[i]), rows, cols)
        k, v = attn_kv(p_att, k_norm_g[i])
        k = axial_rope(k, rows, cols)
        kc, vc = attn_kv(pc_att, k_norm_g[i])
        y_att = window_attention(q, k, v, kc, vc, attn_sink[i])

        y_rw, yc_rw = rwkv_mixer(p_rw, pc_rw, not last, rwkv_mu[i], rwkv_w0[i], rwkv_w_up[i],
                                 rwkv_a0[i], rwkv_a_up[i], rwkv_g_up[i], rwkv_k_k[i],
                                 rwkv_k_a[i], rwkv_r_k[i], rwkv_ln_g[i], rwkv_ln_b[i])

        y_cv = conv_module(p_cv, conv_w[i], conv_b[i], conv_norm_g[i])

        y = jnp.concatenate([y_att, y_rw, y_cv], axis=-1).astype(x.dtype) @ w_out[i]
        x = x + g1 * y

        x = x + g2 * peer_ffn(modulate(x, norm2_g[i], sh2, sc2),
                              peer_wq[i], peer_keys[i], peer_u[i], peer_v[i])

        if not last:
            qc = attn_q(pc_att, q_norm_g[i])
            yc_att = ctx_attention(qc, kc, vc, attn_sink[i])
            yc_cv = conv_module(pc[..., ATT_COLS + RWKV_COLS:], conv_w[i], conv_b[i], conv_norm_g[i])
            yc = jnp.concatenate([yc_att, yc_rw, yc_cv], axis=-1).astype(xc.dtype) @ w_out[i]
            xc = xc + cg1 * yc
            xc = xc + cg2 * peer_ffn(modulate(xc, norm2_g[i], csh2, csc2),
                                     peer_wq[i], peer_keys[i], peer_u[i], peer_v[i])
    return x
```

```python
import functools

import jax
import jax.numpy as jnp
import numpy as np
from jax import lax
from jax.experimental import pallas as pl
from jax.experimental.pallas import tpu as pltpu

F32 = jnp.float32
BF16 = jnp.bfloat16
HIGHEST = lax.Precision.HIGHEST

V7X_LANES = 128
V7X_SUBLANES = 8
V7X_VMEM_BYTES = 64 * 1024 * 1024
VMEM_LIMIT = V7X_VMEM_BYTES - 8 * 1024 * 1024

HEAD_DIM = 64
EPS = 1e-6
GN_EPS = 64e-5
NEG_INF = -1e30
GRID_W = 64
ATT_HEADS = 16
ATT_KV_HEADS = 4
ATT_GROUP = ATT_HEADS // ATT_KV_HEADS
ATT_WIDTH = ATT_HEADS * HEAD_DIM
KV_WIDTH = ATT_KV_HEADS * HEAD_DIM
QK_WIDTH = ATT_WIDTH + KV_WIDTH
ATT_COLS = ATT_WIDTH + 2 * KV_WIDTH
ATT_BLOCK = 128
ROPE_THETA = 10000.0
ROPE_AXIS_DIM = HEAD_DIM // 2
ROPE_HALF = ROPE_AXIS_DIM // 2
RWKV_HEADS = 8
RWKV_WIDTH = RWKV_HEADS * HEAD_DIM
W_LORA = 64
A_LORA = 64
G_LORA = 128
N_DIR = 2
RWKV_COLS = 3 * RWKV_WIDTH + N_DIR * W_LORA + N_DIR * A_LORA + G_LORA
CONV_WIDTH = 512
CONV_KSIZE = 31
CONV_PAD = (CONV_KSIZE - 1) // 2
CONV_HALO = 16
PEER_HEADS = 8
PEER_N_KEYS = 128
PEER_TOPK = 16
N_JVEC = 5
SCAN_ILO = 4
SCAN_IHI = HEAD_DIM // SCAN_ILO


def _cparams(sem):
    return pltpu.CompilerParams(dimension_semantics=sem, vmem_limit_bytes=VMEM_LIMIT)


def _seg_matrices(width, mean):
    nseg = width // HEAD_DIM
    seg = np.arange(width) // HEAD_DIM
    down = np.zeros((width, V7X_LANES), np.float32)
    down[np.arange(width), seg] = 1.0 / HEAD_DIM if mean else 1.0
    up = np.zeros((V7X_LANES, width), np.float32)
    up[seg, np.arange(width)] = 1.0
    assert nseg <= V7X_LANES
    return jnp.asarray(down), jnp.asarray(up)


def _seg_reduce(x, down, up):
    s = jnp.dot(x, down, preferred_element_type=F32, precision=HIGHEST)
    return jnp.dot(s, up, preferred_element_type=F32, precision=HIGHEST)


def _ada_body(c_ref, w_ref, b_ref, o_ref):
    cc = c_ref[...]
    s = (cc * jax.nn.sigmoid(cc)).astype(BF16)
    o_ref[...] = jnp.dot(s, w_ref[...].astype(BF16), preferred_element_type=F32) + b_ref[...]


def _ada(cc, w, b):
    d, n = w.shape
    tn = 1024
    return pl.pallas_call(
        _ada_body,
        grid=(n // tn,),
        in_specs=[pl.BlockSpec((8, d), lambda j: (0, 0)),
                  pl.BlockSpec((d, tn), lambda j: (0, j)),
                  pl.BlockSpec((1, tn), lambda j: (0, j))],
        out_specs=pl.BlockSpec((8, tn), lambda j: (0, j)),
        out_shape=jax.ShapeDtypeStruct((8, n), F32),
        compiler_params=_cparams(("arbitrary",)),
        name="ada",
    )(cc, w, b)


def _modmm_body(x_ref, g_ref, sc_ref, sh_ref, w_ref, o_ref, h_ref):
    @pl.when(pl.program_id(1) == 0)
    def _():
        x = x_ref[...]
        ms = jnp.mean(x * x, axis=-1, keepdims=True)
        y = x * lax.rsqrt(ms + EPS) * g_ref[...]
        h_ref[...] = (y * (1.0 + sc_ref[0]) + sh_ref[0]).astype(BF16)

    o_ref[...] = jnp.dot(h_ref[...], w_ref[...], preferred_element_type=F32)


def _modmm(x, g, sc, sh, w, n_lat, seq, tm, tn):
    m, d = x.shape
    n = w.shape[1]
    n_grp = sc.shape[0] - 1

    def grp(i, j):
        r0 = i * tm
        return (jnp.where(r0 < n_lat, r0 // seq, n_grp), 0, 0)

    return pl.pallas_call(
        _modmm_body,
        grid=(m // tm, n // tn),
        in_specs=[pl.BlockSpec((tm, d), lambda i, j: (i, 0)),
                  pl.BlockSpec((1, d), lambda i, j: (0, 0)),
                  pl.BlockSpec((1, 1, d), grp),
                  pl.BlockSpec((1, 1, d), grp),
                  pl.BlockSpec((d, tn), lambda i, j: (0, j))],
        out_specs=[pl.BlockSpec((tm, tn), lambda i, j: (i, j)),
                   pl.BlockSpec((tm, d), lambda i, j: (i, 0))],
        out_shape=[jax.ShapeDtypeStruct((m, n), F32), jax.ShapeDtypeStruct((m, d), BF16)],
        compiler_params=_cparams(("parallel", "arbitrary")),
        name="modmm",
    )(x, g, sc, sh, w)


def _mm_body(h_ref, w_ref, o_ref):
    o_ref[...] = jnp.dot(h_ref[...], w_ref[...], preferred_element_type=F32)


def _mm(h, w, tm, tn):
    m, d = h.shape
    n = w.shape[1]
    return pl.pallas_call(
        _mm_body,
        grid=(m // tm, n // tn),
        in_specs=[pl.BlockSpec((tm, d), lambda i, j: (i, 0)),
                  pl.BlockSpec((d, tn), lambda i, j: (0, j))],
        out_specs=pl.BlockSpec((tm, tn), lambda i, j: (i, j)),
        out_shape=jax.ShapeDtypeStruct((m, n), F32),
        compiler_params=_cparams(("parallel", "arbitrary")),
        name="mm",
    )(h, w)


def _qkprep_body(p_ref, gain_ref, cos_ref, sin_ref, down_ref, up_ref, qk_ref, v_ref):
    x = p_ref[:, :QK_WIDTH]
    ms = _seg_reduce(x * x, down_ref[...], up_ref[...])
    y = x * lax.rsqrt(ms + EPS) * gain_ref[...]
    cos = cos_ref[...]
    sin = sin_ref[...]
    lane = lax.broadcasted_iota(jnp.int32, cos.shape, 1)
    first = (lane % ROPE_AXIS_DIM) < ROPE_HALF
    for blk in range(QK_WIDTH // V7X_LANES):
        yb = y[:, blk * V7X_LANES:(blk + 1) * V7X_LANES]
        partner = jnp.where(first, pltpu.roll(yb, V7X_LANES - ROPE_HALF, 1), pltpu.roll(yb, ROPE_HALF, 1))
        qk_ref[:, blk * V7X_LANES:(blk + 1) * V7X_LANES] = (yb * cos + partner * sin).astype(BF16)
    v_ref[...] = p_ref[:, QK_WIDTH:].astype(BF16)


def _qkprep(p_att, gain, cos2, sin2, tm):
    m = p_att.shape[0]
    down, up = _seg_matrices(QK_WIDTH, mean=True)
    return pl.pallas_call(
        _qkprep_body,
        grid=(m // tm,),
        in_specs=[pl.BlockSpec((tm, ATT_COLS), lambda i: (i, 0)),
                  pl.BlockSpec((1, QK_WIDTH), lambda i: (0, 0)),
                  pl.BlockSpec((tm, V7X_LANES), lambda i: (i, 0)),
                  pl.BlockSpec((tm, V7X_LANES), lambda i: (i, 0)),
                  pl.BlockSpec(down.shape, lambda i: (0, 0)),
                  pl.BlockSpec(up.shape, lambda i: (0, 0))],
        out_specs=[pl.BlockSpec((tm, QK_WIDTH), lambda i: (i, 0)),
                   pl.BlockSpec((tm, KV_WIDTH), lambda i: (i, 0))],
        out_shape=[jax.ShapeDtypeStruct((m, QK_WIDTH), BF16), jax.ShapeDtypeStruct((m, KV_WIDTH), BF16)],
        compiler_params=_cparams(("parallel",)),
        name="qkprep",
    )(p_att, gain, cos2, sin2, down, up)


def _attn_body(sink_ref, q_ref, k0_ref, k1_ref, k2_ref, v0_ref, v1_ref, v2_ref, kc_ref, vc_ref, o_ref, *, nb):
    n = pl.program_id(1)
    is_lat = n < nb
    rows = ATT_GROUP * ATT_BLOCK
    qi = lax.broadcasted_iota(jnp.int32, (rows, 3 * ATT_BLOCK), 0) % ATT_BLOCK
    si = lax.broadcasted_iota(jnp.int32, (rows, 3 * ATT_BLOCK), 1)
    kblk = n - 1 + si // ATT_BLOCK
    mask = (jnp.abs(si - ATT_BLOCK - qi) <= ATT_BLOCK) & (kblk >= 0) & (kblk < nb) & is_lat
    nt = (((1,), (1,)), ((), ()))
    for g in range(ATT_KV_HEADS):
        ks = slice(g * HEAD_DIM, (g + 1) * HEAD_DIM)
        qs = jnp.concatenate(
            [q_ref[:, (g * ATT_GROUP + a) * HEAD_DIM:(g * ATT_GROUP + a + 1) * HEAD_DIM] for a in range(ATT_GROUP)],
            axis=0)
        kw = jnp.concatenate([k0_ref[:, ks], k1_ref[:, ks], k2_ref[:, ks]], axis=0)
        vw = jnp.concatenate([v0_ref[:, ks], v1_ref[:, ks], v2_ref[:, ks]], axis=0)
        s_win = jnp.where(mask, lax.dot_general(qs, kw, nt, preferred_element_type=F32), NEG_INF)
        s_ctx = lax.dot_general(qs, kc_ref[:, ks], nt, preferred_element_type=F32)
        row_head = lax.broadcasted_iota(jnp.int32, (rows, 1), 0) // ATT_BLOCK
        snk = jnp.zeros((rows, 1), F32)
        for a in range(ATT_GROUP):
            snk = jnp.where(row_head == a, sink_ref[g * ATT_GROUP + a], snk)
        mx = jnp.maximum(jnp.maximum(jnp.max(s_win, axis=-1, keepdims=True),
                                     jnp.max(s_ctx, axis=-1, keepdims=True)), snk)
        p_win = jnp.exp(s_win - mx)
        p_ctx = jnp.exp(s_ctx - mx)
        den = (jnp.sum(p_win, axis=-1, keepdims=True) + jnp.sum(p_ctx, axis=-1, keepdims=True)
               + jnp.exp(snk - mx))
        acc = (jnp.dot(p_win.astype(BF16), vw, preferred_element_type=F32)
               + jnp.dot(p_ctx.astype(BF16), vc_ref[:, ks], preferred_element_type=F32))
        out = acc / den
        for a in range(ATT_GROUP):
            h = g * ATT_GROUP + a
            o_ref[:, h * HEAD_DIM:(h + 1) * HEAD_DIM] = out[a * ATT_BLOCK:(a + 1) * ATT_BLOCK]


def _attention(qk, v, sink, batch, seq, ctx_len):
    m = qk.shape[0]
    nb = seq // ATT_BLOCK
    ncb = ctx_len // ATT_BLOCK
    ctx_blk0 = batch * nb
    kcol = ATT_WIDTH // KV_WIDTH

    def qrow(b, n):
        return jnp.where(n < nb, b * nb + n, ctx_blk0 + b * ncb + (n - nb))

    def krow(off):
        def f(b, n, *_):
            return (b * nb + jnp.clip(n + off, 0, nb - 1), kcol)
        return f

    def vrow(off):
        def f(b, n, *_):
            return (b * nb + jnp.clip(n + off, 0, nb - 1), 0)
        return f

    ctx_row = (batch * seq) // ctx_len
    grid_spec = pltpu.PrefetchScalarGridSpec(
        num_scalar_prefetch=1,
        grid=(batch, nb + ncb),
        in_specs=[pl.BlockSpec((ATT_BLOCK, ATT_WIDTH), lambda b, n, *_: (qrow(b, n), 0)),
                  pl.BlockSpec((ATT_BLOCK, KV_WIDTH), krow(-1)),
                  pl.BlockSpec((ATT_BLOCK, KV_WIDTH), krow(0)),
                  pl.BlockSpec((ATT_BLOCK, KV_WIDTH), krow(1)),
                  pl.BlockSpec((ATT_BLOCK, KV_WIDTH), vrow(-1)),
                  pl.BlockSpec((ATT_BLOCK, KV_WIDTH), vrow(0)),
                  pl.BlockSpec((ATT_BLOCK, KV_WIDTH), vrow(1)),
                  pl.BlockSpec((ctx_len, KV_WIDTH), lambda b, n, *_: (ctx_row + b, kcol)),
                  pl.BlockSpec((ctx_len, KV_WIDTH), lambda b, n, *_: (ctx_row + b, 0))],
        out_specs=pl.BlockSpec((ATT_BLOCK, ATT_WIDTH), lambda b, n, *_: (qrow(b, n), 0)),
    )
    return pl.pallas_call(
        functools.partial(_attn_body, nb=nb),
        grid_spec=grid_spec,
        out_shape=jax.ShapeDtypeStruct((m, ATT_WIDTH), F32),
        compiler_params=_cparams(("parallel", "arbitrary")),
        name="attention",
    )(sink, qk, qk, qk, qk, v, v, v, qk, v)


def _seq_edges(i, tm, n_lat, seq, ctx_len):
    r0 = i * tm
    in_lat = r0 < n_lat
    pos = jnp.where(in_lat, r0 % seq, (r0 - n_lat) % ctx_len)
    length = jnp.where(in_lat, seq, ctx_len)
    return pos == 0, pos + tm == length


def _halo_specs(tm, halo, width, n_rows):
    per = tm // halo
    last = n_rows // halo - 1
    prev = pl.BlockSpec((halo, width), lambda i: (jnp.maximum(i * per - 1, 0), 0))
    nxt = pl.BlockSpec((halo, width), lambda i: (jnp.minimum((i + 1) * per, last), 0))
    return prev, nxt


def _rwkvfeat_body(p_ref, pv_ref, nx_ref, mu_ref, w0_ref, wup_ref, a0_ref, aup_ref, gup_ref, kk_ref, ka_ref,
                   down_ref, up_ref, jv_ref, misc_ref, ext_ref, *, tm, n_lat, seq, ctx_len):
    first, last = _seq_edges(pl.program_id(0), tm, n_lat, seq, ctx_len)
    h = V7X_SUBLANES
    ext_ref[h:h + tm, :] = p_ref[...]
    ext_ref[0:h, :] = jnp.where(first, 0.0, pv_ref[...])
    ext_ref[h + tm:h + tm + h, :] = jnp.where(last, 0.0, nx_ref[...])
    p = p_ref[...]
    prev = ext_ref[h - 1:h - 1 + tm, :]
    nxt = ext_ref[h + 1:h + 1 + tm, :]
    xs = p + mu_ref[...] * (0.5 * (prev + nxt) - p)
    W = RWKV_WIDTH
    r = xs[:, 0:W]
    k = xs[:, W:2 * W]
    v = xs[:, 2 * W:3 * W]
    o = 3 * W
    wd = xs[:, o:o + N_DIR * W_LORA]
    ad = xs[:, o + N_DIR * W_LORA:o + N_DIR * (W_LORA + A_LORA)]
    gd = xs[:, o + N_DIR * (W_LORA + A_LORA):]
    w_lin = w0_ref[...] + jnp.dot(jnp.tanh(wd), wup_ref[...], preferred_element_type=F32, precision=HIGHEST)
    w_log = -jax.nn.softplus(-w_lin) - 0.5
    decay = jnp.exp(-jnp.exp(w_log))
    a = jax.nn.sigmoid(a0_ref[...] + jnp.dot(ad, aup_ref[...], preferred_element_type=F32, precision=HIGHEST))
    g = jnp.dot(jax.nn.sigmoid(gd), gup_ref[...], preferred_element_type=F32, precision=HIGHEST)
    kk = k * kk_ref[...]
    kk = kk * lax.rsqrt(_seg_reduce(kk * kk, down_ref[...], up_ref[...]) + 1e-12)
    for d in range(N_DIR):
        a_d = a[:, d * W:(d + 1) * W]
        base = d * N_JVEC * W
        jv_ref[:, base + 0 * W:base + 1 * W] = decay[:, d * W:(d + 1) * W]
        jv_ref[:, base + 1 * W:base + 2 * W] = -kk
        jv_ref[:, base + 2 * W:base + 3 * W] = kk * a_d
        jv_ref[:, base + 3 * W:base + 4 * W] = k * (1.0 + (a_d - 1.0) * ka_ref[...])
        jv_ref[:, base + 4 * W:base + 5 * W] = r
    misc_ref[:, 0:W] = r
    misc_ref[:, W:2 * W] = k
    misc_ref[:, 2 * W:3 * W] = v
    misc_ref[:, 3 * W:4 * W] = g


def _rwkvfeat(p_rw, mu, w0, wup2, a0, aup2, gup, k_k, k_a, tm, n_lat, seq, ctx_len):
    m = p_rw.shape[0]
    down, up = _seg_matrices(RWKV_WIDTH, mean=False)
    prev, nxt = _halo_specs(tm, V7X_SUBLANES, RWKV_COLS, m)
    full = lambda a: pl.BlockSpec(a.shape, lambda i: (0,) * a.ndim)
    params = [mu, w0, wup2, a0, aup2, gup, k_k, k_a, down, up]
    return pl.pallas_call(
        functools.partial(_rwkvfeat_body, tm=tm, n_lat=n_lat, seq=seq, ctx_len=ctx_len),
        grid=(m // tm,),
        in_specs=[pl.BlockSpec((tm, RWKV_COLS), lambda i: (i, 0)), prev, nxt] + [full(a) for a in params],
        out_specs=[pl.BlockSpec((tm, N_DIR * N_JVEC * RWKV_WIDTH), lambda i: (i, 0)),
                   pl.BlockSpec((tm, 4 * RWKV_WIDTH), lambda i: (i, 0))],
        out_shape=[jax.ShapeDtypeStruct((m, N_DIR * N_JVEC * RWKV_WIDTH), F32),
                   jax.ShapeDtypeStruct((m, 4 * RWKV_WIDTH), F32)],
        scratch_shapes=[pltpu.VMEM((tm + 2 * V7X_SUBLANES, RWKV_COLS), F32)],
        compiler_params=_cparams(("parallel",)),
        name="rwkvfeat",
    )(p_rw, p_rw, p_rw, *params)


def _scan_body(j_ref, v_ref, y_ref, s_ref, *, ts):
    @pl.when(pl.program_id(0) == 0)
    def _():
        s_ref[...] = jnp.zeros_like(s_ref)

    n_acc = 4

    def step(s, carry):
        v = v_ref[s]
        parts = [None] * n_acc
        for j in range(HEAD_DIM):
            t = s_ref[j] * j_ref[s, 1, pl.ds(j, 1), :]
            parts[j % n_acc] = t if parts[j % n_acc] is None else parts[j % n_acc] + t
        sa = (parts[0] + parts[1]) + (parts[2] + parts[3])
        yparts = [None] * n_acc
        for j in range(HEAD_DIM):
            sj = (s_ref[j] * j_ref[s, 0, pl.ds(j, 1), :] + sa * j_ref[s, 2, pl.ds(j, 1), :]
                  + v * j_ref[s, 3, pl.ds(j, 1), :])
            s_ref[j] = sj
            t = sj * j_ref[s, 4, pl.ds(j, 1), :]
            yparts[j % n_acc] = t if yparts[j % n_acc] is None else yparts[j % n_acc] + t
        y_ref[s] = (yparts[0] + yparts[1]) + (yparts[2] + yparts[3])
        return carry

    lax.fori_loop(0, ts, step, 0)


def _scan(jv, vv, ts):
    steps = jv.shape[0]
    return pl.pallas_call(
        functools.partial(_scan_body, ts=ts),
        grid=(steps // ts,),
        in_specs=[pl.BlockSpec((ts, N_JVEC, HEAD_DIM, V7X_LANES), lambda i: (i, 0, 0, 0)),
                  pl.BlockSpec((ts, SCAN_IHI, V7X_LANES), lambda i: (i, 0, 0))],
        out_specs=pl.BlockSpec((ts, SCAN_IHI, V7X_LANES), lambda i: (i, 0, 0)),
        out_shape=jax.ShapeDtypeStruct((steps, SCAN_IHI, V7X_LANES), F32),
        scratch_shapes=[pltpu.VMEM((HEAD_DIM, SCAN_IHI, V7X_LANES), F32)],
        compiler_params=_cparams(("arbitrary",)),
        name="scan",
    )(jv, vv)


def _to_steps(a, batch, seq, ctx_len, reverse):
    n_lat = batch * seq
    lat = a[:n_lat].reshape((batch, seq) + a.shape[1:])
    ctx = a[n_lat:].reshape((batch, ctx_len) + a.shape[1:])
    if reverse:
        lat, ctx = lat[:, ::-1], ctx[:, ::-1]
    return jnp.concatenate([ctx, lat], axis=1)


def _from_steps(y, batch, seq, ctx_len, reverse):
    ctx, lat = y[:, :ctx_len], y[:, ctx_len:]
    if reverse:
        lat, ctx = lat[:, ::-1], ctx[:, ::-1]
    return jnp.concatenate([lat.reshape((batch * seq,) + y.shape[2:]),
                            ctx.reshape((batch * ctx_len,) + y.shape[2:])], axis=0)


def _rwkv_scan(jv, misc, batch, seq, ctx_len, ts):
    m = jv.shape[0]
    steps = seq + ctx_len
    jv = jv.reshape(m, N_DIR, N_JVEC, RWKV_HEADS, HEAD_DIM)
    v = misc[:, 2 * RWKV_WIDTH:3 * RWKV_WIDTH].reshape(m, RWKV_HEADS, SCAN_IHI, SCAN_ILO)
    js = jnp.stack([_to_steps(jv[:, d], batch, seq, ctx_len, d == 1) for d in range(N_DIR)])
    js = jnp.transpose(js, (2, 3, 5, 0, 1, 4))
    js = jnp.broadcast_to(js[..., None], js.shape + (SCAN_ILO,)).reshape(steps, N_JVEC, HEAD_DIM, V7X_LANES)
    vs = jnp.stack([_to_steps(v, batch, seq, ctx_len, d == 1) for d in range(N_DIR)])
    vs = jnp.transpose(vs, (2, 4, 0, 1, 3, 5)).reshape(steps, SCAN_IHI, V7X_LANES)
    y = _scan(js, vs, ts)
    y = y.reshape(steps, SCAN_IHI, N_DIR, batch, RWKV_HEADS, SCAN_ILO)
    y = jnp.transpose(y, (2, 3, 0, 4, 1, 5)).reshape(N_DIR, batch, steps, RWKV_WIDTH)
    return jnp.stack([_from_steps(y[d], batch, seq, ctx_len, d == 1) for d in range(N_DIR)])


def _rwkvout_body(y_ref, misc_ref, rk_ref, lg_ref, lb_ref, dmean_ref, dsum_ref, up_ref, o_ref):
    W = RWKV_WIDTH
    y = y_ref[0] + y_ref[1]
    r = misc_ref[:, 0:W]
    k = misc_ref[:, W:2 * W]
    v = misc_ref[:, 2 * W:3 * W]
    g = misc_ref[:, 3 * W:4 * W]
    mean = _seg_reduce(y, dmean_ref[...], up_ref[...])
    yc = y - mean
    var = _seg_reduce(yc * yc, dmean_ref[...], up_ref[...])
    yn = yc * lax.rsqrt(var + GN_EPS) * lg_ref[...] + lb_ref[...]
    bonus = _seg_reduce(r * k * rk_ref[...], dsum_ref[...], up_ref[...]) * v
    o_ref[...] = (yn + bonus) * g


def _rwkvout(y2, misc, r_k, ln_g, ln_b, tm):
    m = misc.shape[0]
    dmean, up = _seg_matrices(RWKV_WIDTH, mean=True)
    dsum, _ = _seg_matrices(RWKV_WIDTH, mean=False)
    full = lambda a: pl.BlockSpec(a.shape, lambda i: (0,) * a.ndim)
    params = [r_k, ln_g, ln_b, dmean, dsum, up]
    return pl.pallas_call(
        _rwkvout_body,
        grid=(m // tm,),
        in_specs=[pl.BlockSpec((N_DIR, tm, RWKV_WIDTH), lambda i: (0, i, 0)),
                  pl.BlockSpec((tm, 4 * RWKV_WIDTH), lambda i: (i, 0))] + [full(a) for a in params],
        out_specs=pl.BlockSpec((tm, RWKV_WIDTH), lambda i: (i, 0)),
        out_shape=jax.ShapeDtypeStruct((m, RWKV_WIDTH), F32),
        compiler_params=_cparams(("parallel",)),
        name="rwkvout",
    )(y2, misc, *params)


def _conv_body(p_ref, pv_ref, nx_ref, w_ref, b_ref, g_ref, o_ref, ext_ref, *, tm, n_lat, seq, ctx_len):
    first, last = _seq_edges(pl.program_id(0), tm, n_lat, seq, ctx_len)
    C = CONV_WIDTH
    H = CONV_HALO

    def glu(t):
        return t[:, :C] * jax.nn.sigmoid(t[:, C:])

    ext_ref[H:H + tm, :] = glu(p_ref[...])
    ext_ref[0:H, :] = jnp.where(first, 0.0, glu(pv_ref[...]))
    ext_ref[H + tm:H + tm + H, :] = jnp.where(last, 0.0, glu(nx_ref[...]))
    rows = 64
    for c in range(tm // rows):
        acc = jnp.zeros((rows, C), F32)
        for t in range(CONV_KSIZE):
            off = c * rows + H - CONV_PAD + t
            acc = acc + ext_ref[off:off + rows, :] * w_ref[t:t + 1, :]
        u = acc + b_ref[...]
        ms = jnp.mean(u * u, axis=-1, keepdims=True)
        z = u * lax.rsqrt(ms + EPS) * g_ref[...]
        o_ref[c * rows:(c + 1) * rows, :] = z * jax.nn.sigmoid(z)


def _conv(p_cv, w, b, g, tm, n_lat, seq, ctx_len):
    m = p_cv.shape[0]
    prev, nxt = _halo_specs(tm, CONV_HALO, 2 * CONV_WIDTH, m)
    full = lambda a: pl.BlockSpec(a.shape, lambda i: (0,) * a.ndim)
    return pl.pallas_call(
        functools.partial(_conv_body, tm=tm, n_lat=n_lat, seq=seq, ctx_len=ctx_len),
        grid=(m // tm,),
        in_specs=[pl.BlockSpec((tm, 2 * CONV_WIDTH), lambda i: (i, 0)), prev, nxt, full(w), full(b), full(g)],
        out_specs=pl.BlockSpec((tm, CONV_WIDTH), lambda i: (i, 0)),
        out_shape=jax.ShapeDtypeStruct((m, CONV_WIDTH), F32),
        scratch_shapes=[pltpu.VMEM((tm + 2 * CONV_HALO, CONV_WIDTH), F32)],
        compiler_params=_cparams(("parallel",)),
        name="conv",
    )(p_cv, p_cv, p_cv, w, b, g)


def _outproj_body(ya_ref, yr_ref, yc_ref, x_ref, g_ref, wa_ref, wr_ref, wc_ref, o_ref):
    acc = jnp.dot(ya_ref[...].astype(BF16), wa_ref[...], preferred_element_type=F32)
    acc += jnp.dot(yr_ref[...].astype(BF16), wr_ref[...], preferred_element_type=F32)
    acc += jnp.dot(yc_ref[...].astype(BF16), wc_ref[...], preferred_element_type=F32)
    o_ref[...] = x_ref[...] + g_ref[0] * acc


def _outproj(ya, yr, yc, x, gate, w, n_lat, seq, tm, tn):
    m, d = x.shape
    n_grp = gate.shape[0] - 1

    def grp(i, j):
        r0 = i * tm
        return (jnp.where(r0 < n_lat, r0 // seq, n_grp), 0, j)

    wa, wr, wc = w[:ATT_WIDTH], w[ATT_WIDTH:ATT_WIDTH + RWKV_WIDTH], w[ATT_WIDTH + RWKV_WIDTH:]
    return pl.pallas_call(
        _outproj_body,
        grid=(m // tm, d // tn),
        in_specs=[pl.BlockSpec((tm, ATT_WIDTH), lambda i, j: (i, 0)),
                  pl.BlockSpec((tm, RWKV_WIDTH), lambda i, j: (i, 0)),
                  pl.BlockSpec((tm, CONV_WIDTH), lambda i, j: (i, 0)),
                  pl.BlockSpec((tm, tn), lambda i, j: (i, j)),
                  pl.BlockSpec((1, 1, tn), grp),
                  pl.BlockSpec((ATT_WIDTH, tn), lambda i, j: (0, j)),
                  pl.BlockSpec((RWKV_WIDTH, tn), lambda i, j: (0, j)),
                  pl.BlockSpec((CONV_WIDTH, tn), lambda i, j: (0, j))],
        out_specs=pl.BlockSpec((tm, tn), lambda i, j: (i, j)),
        out_shape=jax.ShapeDtypeStruct((m, d), F32),
        compiler_params=_cparams(("parallel", "arbitrary")),
        name="outproj",
    )(ya, yr, yc, x, gate, wa, wr, wc)


def _peer_pairs():
    n = PEER_TOPK + 1
    return [(a, b) for a in range(n) for b in range(n) if (a + 1) * (b + 1) <= n]


def _top_distinct(s, n):
    vals, cnts = [], []
    v = s
    for _ in range(n):
        mx = jnp.max(v, axis=0, keepdims=True)
        hit = v == mx
        cnt = jnp.sum(hit.astype(F32), axis=0, keepdims=True)
        vals.append(mx)
        cnts.append(jnp.where(mx == -jnp.inf, 0.0, cnt))
        v = jnp.where(hit, -jnp.inf, v)
    return vals, cnts


def _peertopk_body(q_ref, keys_ref, s2_ref, e2_ref, a1_ref, e1_ref):
    n = PEER_TOPK + 1
    nt = (((1,), (1,)), ((), ()))
    pairs = _peer_pairs()
    for h in range(PEER_HEADS):
        sc = []
        for p in range(2):
            c0 = (h * 2 + p) * PEER_N_KEYS
            sc.append(lax.dot_general(keys_ref[h, p], q_ref[:, c0:c0 + PEER_N_KEYS], nt,
                                      preferred_element_type=F32, precision=HIGHEST))
        (v1, n1), (v2, n2) = _top_distinct(sc[0], n), _top_distinct(sc[1], n)
        cv = jnp.concatenate([v1[a] + v2[b] for a, b in pairs], axis=0)
        cm = jnp.concatenate([n1[a] * n2[b] for a, b in pairs], axis=0)
        top = v1[0] + v2[0]
        cum = jnp.zeros_like(top)
        t16 = jnp.full_like(top, -jnp.inf)
        t17 = jnp.full_like(top, -jnp.inf)
        z = jnp.zeros_like(top)
        for _ in range(n):
            mx = jnp.max(cv, axis=0, keepdims=True)
            hit = cv == mx
            cnt = jnp.where(mx == -jnp.inf, 0.0, jnp.sum(jnp.where(hit, cm, 0.0), axis=0, keepdims=True))
            new = cum + cnt
            t16 = jnp.where((cum < PEER_TOPK) & (new >= PEER_TOPK), mx, t16)
            t17 = jnp.where((cum < n) & (new >= n), mx, t17)
            take = jnp.minimum(cnt, jnp.maximum(PEER_TOPK - cum, 0.0))
            z = z + jnp.where(take > 0, take * jnp.exp(mx - top), 0.0)
            cum = new
            cv = jnp.where(hit, -jnp.inf, cv)
        thr = 0.5 * (t16 + t17)
        s2_ref[h] = sc[1]
        e2_ref[h] = jnp.exp(sc[1] - v2[0])
        a1_ref[h] = thr - sc[0]
        e1_ref[h] = jnp.exp(sc[0] - v1[0]) / z


def _peertopk(q, keys, tt):
    m, qw = q.shape
    shp = jax.ShapeDtypeStruct((PEER_HEADS, PEER_N_KEYS, m), F32)
    ospec = pl.BlockSpec((PEER_HEADS, PEER_N_KEYS, tt), lambda i: (0, 0, i))
    return pl.pallas_call(
        _peertopk_body,
        grid=(m // tt,),
        in_specs=[pl.BlockSpec((tt, qw), lambda i: (i, 0)),
                  pl.BlockSpec(keys.shape, lambda i: (0, 0, 0, 0))],
        out_specs=[ospec] * 4,
        out_shape=[shp] * 4,
        compiler_params=_cparams(("parallel",)),
        name="peertopk",
    )(q, keys)


def _peerdense_body(h_ref, u_ref, vt_ref, s2_ref, e2_ref, a1_ref, e1_ref, x_ref, g_ref, o_ref, acc_ref, a_ref, *, ec):
    e = pl.program_id(1)

    @pl.when(e == 0)
    def _():
        acc_ref[...] = jnp.zeros_like(acc_ref)

    nt = (((1,), (1,)), ((), ()))
    st = lax.dot_general(u_ref[...], h_ref[...], nt, preferred_element_type=F32)
    act = 0.5 * st * (1.0 + lax.erf(st * np.float32(1.0 / np.sqrt(2.0))))
    per = ec // PEER_N_KEYS
    for ii in range(per):
        i1 = e * per + ii
        w = None
        for h in range(PEER_HEADS):
            sel = s2_ref[h] >= a1_ref[h, pl.ds(i1, 1), :]
            t = jnp.where(sel, e2_ref[h] * e1_ref[h, pl.ds(i1, 1), :], 0.0)
            w = t if w is None else w + t
        rs = slice(ii * PEER_N_KEYS, (ii + 1) * PEER_N_KEYS)
        a_ref[rs, :] = (act[rs] * w).astype(BF16)
    acc_ref[...] += jnp.dot(vt_ref[...], a_ref[...], preferred_element_type=F32)

    @pl.when(e == pl.num_programs(1) - 1)
    def _():
        o_ref[...] = x_ref[...] + g_ref[0] * acc_ref[...].T


def _peerdense(h2, u, vt, s2, e2, a1, e1, x, gate, n_lat, seq, tt, ec):
    m, d = x.shape
    n_exp = u.shape[0]
    n_grp = gate.shape[0] - 1

    def grp(i, e):
        r0 = i * tt
        return (jnp.where(r0 < n_lat, r0 // seq, n_grp), 0, 0)

    once = pl.Buffered(1)
    tok = pl.BlockSpec((PEER_HEADS, PEER_N_KEYS, tt), lambda i, e: (0, 0, i), pipeline_mode=once)
    return pl.pallas_call(
        functools.partial(_peerdense_body, ec=ec),
        grid=(m // tt, n_exp // ec),
        in_specs=[pl.BlockSpec((tt, d), lambda i, e: (i, 0), pipeline_mode=once),
                  pl.BlockSpec((ec, d), lambda i, e: (e, 0)),
                  pl.BlockSpec((d, ec), lambda i, e: (0, e)),
                  tok, tok, tok, tok,
                  pl.BlockSpec((tt, d), lambda i, e: (i, 0), pipeline_mode=once),
                  pl.BlockSpec((1, 1, d), grp)],
        out_specs=pl.BlockSpec((tt, d), lambda i, e: (i, 0)),
        out_shape=jax.ShapeDtypeStruct((m, d), F32),
        scratch_shapes=[pltpu.VMEM((d, tt), F32), pltpu.VMEM((ec, tt), BF16)],
        compiler_params=_cparams(("parallel", "arbitrary")),
        name="peerdense",
    )(h2, u, vt, s2, e2, a1, e1, x, gate)


def _tiles(seq, ctx_len, batch):
    ctx_rows = batch * ctx_len
    return dict(
        mm=min(512, seq, ctx_rows),
        seqt=min(256, seq, ctx_len),
        peer=min(512, seq, ctx_rows),
        scan=32,
    )


def _rope_tables(batch, seq, ctx_len):
    inv = ROPE_THETA ** (-jnp.arange(0, ROPE_AXIS_DIM, 2, dtype=F32) / ROPE_AXIS_DIM)
    t = jnp.arange(seq, dtype=jnp.int32)
    ang_r = (t // GRID_W).astype(F32)[:, None] * inv[None, :]
    ang_c = (t % GRID_W).astype(F32)[:, None] * inv[None, :]
    cos = jnp.concatenate([jnp.cos(ang_r)] * 2 + [jnp.cos(ang_c)] * 2, axis=-1)
    sin = jnp.concatenate([-jnp.sin(ang_r), jnp.sin(ang_r), -jnp.sin(ang_c), jnp.sin(ang_c)], axis=-1)
    n_ctx = batch * ctx_len
    cos = jnp.concatenate([jnp.tile(cos, (batch, 1)), jnp.ones((n_ctx, HEAD_DIM), F32)], axis=0)
    sin = jnp.concatenate([jnp.tile(sin, (batch, 1)), jnp.zeros((n_ctx, HEAD_DIM), F32)], axis=0)
    return jnp.tile(cos, (1, 2)), jnp.tile(sin, (1, 2))


def _block_diag2(w):
    z = jnp.zeros_like(w[0])
    return jnp.concatenate([jnp.concatenate([w[0], z], axis=1), jnp.concatenate([z, w[1]], axis=1)], axis=0)


def kernel(x, c, ctx, c_ctx, norm1_g, norm2_g, w_ada, b_ada, w_in, w_out, q_norm_g, k_norm_g, attn_sink, rwkv_mu, rwkv_w0, rwkv_w_up, rwkv_a0, rwkv_a_up, rwkv_g_up, rwkv_k_k, rwkv_k_a, rwkv_r_k, rwkv_ln_g, rwkv_ln_b, conv_w, conv_b, conv_norm_g, peer_wq, peer_keys, peer_u, peer_v):
    batch, seq, d = x.shape
    ctx_len = ctx.shape[1]
    depth = w_in.shape[0]
    n_lat = batch * seq
    tl = _tiles(seq, ctx_len, batch)
    assert seq % ATT_BLOCK == 0 and ctx_len % ATT_BLOCK == 0 and n_lat % ctx_len == 0
    assert (seq + ctx_len) % tl["scan"] == 0

    xt = jnp.concatenate([x.reshape(n_lat, d), ctx.reshape(batch * ctx_len, d)], axis=0)
    cc = jnp.concatenate([c, c_ctx[None, :], jnp.zeros((8 - batch - 1, d), F32)], axis=0)
    cos2, sin2 = _rope_tables(batch, seq, ctx_len)
    row = lambda a: a.reshape(1, -1)

    for i in range(depth):
        mod = _ada(cc, w_ada[i], row(b_ada[i]))[:batch + 1].reshape(batch + 1, 6, 1, d)
        sh1, sc1, g1, sh2, sc2, g2 = (mod[:, k] for k in range(6))

        w_in_b = w_in[i].astype(BF16)
        p_att, h1 = _modmm(xt, row(norm1_g[i]), sc1, sh1, w_in_b[:, :ATT_COLS], n_lat, seq, tl["mm"], 768)
        p_rw = _mm(h1, w_in_b[:, ATT_COLS:ATT_COLS + RWKV_COLS], tl["mm"], 640)
        p_cv = _mm(h1, w_in_b[:, ATT_COLS + RWKV_COLS:], tl["mm"], 1024)

        gain = jnp.concatenate([jnp.tile(q_norm_g[i] * HEAD_DIM ** -0.5, ATT_HEADS), jnp.tile(k_norm_g[i], ATT_KV_HEADS)])
        qk, vv = _qkprep(p_att, row(gain), cos2, sin2, tl["seqt"])
        y_att = _attention(qk, vv, attn_sink[i], batch, seq, ctx_len)

        jv, misc = _rwkvfeat(p_rw, row(rwkv_mu[i]), row(rwkv_w0[i]), _block_diag2(rwkv_w_up[i]),
                             row(rwkv_a0[i]), _block_diag2(rwkv_a_up[i]), rwkv_g_up[i],
                             row(rwkv_k_k[i]), row(rwkv_k_a[i]), tl["seqt"], n_lat, seq, ctx_len)
        y2 = _rwkv_scan(jv, misc, batch, seq, ctx_len, tl["scan"])
        y_rw = _rwkvout(y2, misc, row(rwkv_r_k[i]), row(rwkv_ln_g[i]), row(rwkv_ln_b[i]), tl["seqt"])

        y_cv = _conv(p_cv, conv_w[i], row(conv_b[i]), row(conv_norm_g[i]), tl["seqt"], n_lat, seq, ctx_len)

        xt = _outproj(y_att, y_rw, y_cv, xt, g1, w_out[i].astype(BF16), n_lat, seq, tl["mm"], 1024)

        q, h2 = _modmm(xt, row(norm2_g[i]), sc2, sh2, peer_wq[i].astype(BF16), n_lat, seq, tl["mm"], 1024)
        s2, e2, a1, e1 = _peertopk(q, peer_keys[i], tl["peer"])
        xt = _peerdense(h2, peer_u[i].astype(BF16), peer_v[i].T.astype(BF16), s2, e2, a1, e1, xt, g2,
                        n_lat, seq, tl["peer"], 1024)
    return xt[:n_lat].reshape(batch, seq, d)
```

```python
import functools

import jax
import jax.numpy as jnp
import numpy as np
from jax import lax
from jax.experimental import pallas as pl
from jax.experimental.pallas import tpu as pltpu

F32 = jnp.float32
BF16 = jnp.bfloat16
HIGHEST = lax.Precision.HIGHEST

V7X_LANES = 128
V7X_SUBLANES = 8
V7X_VMEM_BYTES = 64 * 1024 * 1024
VMEM_LIMIT = V7X_VMEM_BYTES - 8 * 1024 * 1024

HEAD_DIM = 64
EPS = 1e-6
GN_EPS = 64e-5
NEG_INF = -1e30
GRID_W = 64
ATT_HEADS = 16
ATT_KV_HEADS = 4
ATT_GROUP = ATT_HEADS // ATT_KV_HEADS
ATT_WIDTH = ATT_HEADS * HEAD_DIM
KV_WIDTH = ATT_KV_HEADS * HEAD_DIM
QK_WIDTH = ATT_WIDTH + KV_WIDTH
ATT_COLS = ATT_WIDTH + 2 * KV_WIDTH
ATT_BLOCK = 128
ROPE_THETA = 10000.0
ROPE_AXIS_DIM = HEAD_DIM // 2
ROPE_HALF = ROPE_AXIS_DIM // 2
RWKV_HEADS = 8
RWKV_WIDTH = RWKV_HEADS * HEAD_DIM
W_LORA = 64
A_LORA = 64
G_LORA = 128
N_DIR = 2
RWKV_COLS = 3 * RWKV_WIDTH + N_DIR * W_LORA + N_DIR * A_LORA + G_LORA
CONV_WIDTH = 512
CONV_KSIZE = 31
CONV_PAD = (CONV_KSIZE - 1) // 2
CONV_HALO = 16
PEER_HEADS = 8
PEER_N_KEYS = 128
PEER_TOPK = 16
N_JVEC = 5
SCAN_ILO = 4
SCAN_IHI = HEAD_DIM // SCAN_ILO


def _cparams(sem):
    return pltpu.CompilerParams(dimension_semantics=sem, vmem_limit_bytes=VMEM_LIMIT)


def _seg_matrices(width, mean):
    nseg = width // HEAD_DIM
    seg = np.arange(width) // HEAD_DIM
    down = np.zeros((width, V7X_LANES), np.float32)
    down[np.arange(width), seg] = 1.0 / HEAD_DIM if mean else 1.0
    up = np.zeros((V7X_LANES, width), np.float32)
    up[seg, np.arange(width)] = 1.0
    assert nseg <= V7X_LANES
    return jnp.asarray(down), jnp.asarray(up)


def _seg_reduce(x, down, up):
    s = jnp.dot(x, down, preferred_element_type=F32, precision=HIGHEST)
    return jnp.dot(s, up, preferred_element_type=F32, precision=HIGHEST)


def _ada_body(c_ref, w_ref, b_ref, o_ref):
    cc = c_ref[...]
    s = (cc * jax.nn.sigmoid(cc)).astype(BF16)
    o_ref[...] = jnp.dot(s, w_ref[...].astype(BF16), preferred_element_type=F32) + b_ref[...]


def _ada(cc, w, b):
    d, n = w.shape
    tn = 1024
    return pl.pallas_call(
        _ada_body,
        grid=(n // tn,),
        in_specs=[pl.BlockSpec((8, d), lambda j: (0, 0)),
                  pl.BlockSpec((d, tn), lambda j: (0, j)),
                  pl.BlockSpec((1, tn), lambda j: (0, j))],
        out_specs=pl.BlockSpec((8, tn), lambda j: (0, j)),
        out_shape=jax.ShapeDtypeStruct((8, n), F32),
        compiler_params=_cparams(("arbitrary",)),
        name="ada",
    )(cc, w, b)


def _modmm_body(x_ref, g_ref, sc_ref, sh_ref, w_ref, o_ref, h_ref):
    @pl.when(pl.program_id(1) == 0)
    def _():
        x = x_ref[...]
        ms = jnp.mean(x * x, axis=-1, keepdims=True)
        y = x * lax.rsqrt(ms + EPS) * g_ref[...]
        h_ref[...] = (y * (1.0 + sc_ref[0]) + sh_ref[0]).astype(BF16)

    o_ref[...] = jnp.dot(h_ref[...], w_ref[...], preferred_element_type=F32)


def _modmm(x, g, sc, sh, w, n_lat, seq, tm, tn):
    m, d = x.shape
    n = w.shape[1]
    n_grp = sc.shape[0] - 1

    def grp(i, j):
        r0 = i * tm
        return (jnp.where(r0 < n_lat, r0 // seq, n_grp), 0, 0)

    return pl.pallas_call(
        _modmm_body,
        grid=(m // tm, n // tn),
        in_specs=[pl.BlockSpec((tm, d), lambda i, j: (i, 0)),
                  pl.BlockSpec((1, d), lambda i, j: (0, 0)),
                  pl.BlockSpec((1, 1, d), grp),
                  pl.BlockSpec((1, 1, d), grp),
                  pl.BlockSpec((d, tn), lambda i, j: (0, j))],
        out_specs=[pl.BlockSpec((tm, tn), lambda i, j: (i, j)),
                   pl.BlockSpec((tm, d), lambda i, j: (i, 0))],
        out_shape=[jax.ShapeDtypeStruct((m, n), F32), jax.ShapeDtypeStruct((m, d), BF16)],
        compiler_params=_cparams(("parallel", "arbitrary")),
        name="modmm",
    )(x, g, sc, sh, w)


def _mm_body(h_ref, w_ref, o_ref):
    o_ref[...] = jnp.dot(h_ref[...], w_ref[...], preferred_element_type=F32)


def _mm(h, w, tm, tn):
    m, d = h.shape
    n = w.shape[1]
    return pl.pallas_call(
        _mm_body,
        grid=(m // tm, n // tn),
        in_specs=[pl.BlockSpec((tm, d), lambda i, j: (i, 0)),
                  pl.BlockSpec((d, tn), lambda i, j: (0, j))],
        out_specs=pl.BlockSpec((tm, tn), lambda i, j: (i, j)),
        out_shape=jax.ShapeDtypeStruct((m, n), F32),
        compiler_params=_cparams(("parallel", "arbitrary")),
        name="mm",
    )(h, w)


def _qkprep_body(p_ref, gain_ref, cos_ref, sin_ref, down_ref, up_ref, qk_ref, v_ref):
    x = p_ref[:, :QK_WIDTH]
    ms = _seg_reduce(x * x, down_ref[...], up_ref[...])
    y = x * lax.rsqrt(ms + EPS) * gain_ref[...]
    cos = cos_ref[...]
    sin = sin_ref[...]
    lane = lax.broadcasted_iota(jnp.int32, cos.shape, 1)
    first = (lane % ROPE_AXIS_DIM) < ROPE_HALF
    for blk in range(QK_WIDTH // V7X_LANES):
        yb = y[:, blk * V7X_LANES:(blk + 1) * V7X_LANES]
        partner = jnp.where(first, pltpu.roll(yb, V7X_LANES - ROPE_HALF, 1), pltpu.roll(yb, ROPE_HALF, 1))
        qk_ref[:, blk * V7X_LANES:(blk + 1) * V7X_LANES] = (yb * cos + partner * sin).astype(BF16)
    v_ref[...] = p_ref[:, QK_WIDTH:].astype(BF16)


def _qkprep(p_att, gain, cos2, sin2, tm):
    m = p_att.shape[0]
    down, up = _seg_matrices(QK_WIDTH, mean=True)
    return pl.pallas_call(
        _qkprep_body,
        grid=(m // tm,),
        in_specs=[pl.BlockSpec((tm, ATT_COLS), lambda i: (i, 0)),
                  pl.BlockSpec((1, QK_WIDTH), lambda i: (0, 0)),
                  pl.BlockSpec((tm, V7X_LANES), lambda i: (i, 0)),
                  pl.BlockSpec((tm, V7X_LANES), lambda i: (i, 0)),
                  pl.BlockSpec(down.shape, lambda i: (0, 0)),
                  pl.BlockSpec(up.shape, lambda i: (0, 0))],
        out_specs=[pl.BlockSpec((tm, QK_WIDTH), lambda i: (i, 0)),
                   pl.BlockSpec((tm, KV_WIDTH), lambda i: (i, 0))],
        out_shape=[jax.ShapeDtypeStruct((m, QK_WIDTH), BF16), jax.ShapeDtypeStruct((m, KV_WIDTH), BF16)],
        compiler_params=_cparams(("parallel",)),
        name="qkprep",
    )(p_att, gain, cos2, sin2, down, up)


def _attn_body(sink_ref, q_ref, k0_ref, k1_ref, k2_ref, v0_ref, v1_ref, v2_ref, kc_ref, vc_ref, o_ref, *, nb):
    n = pl.program_id(1)
    is_lat = n < nb
    rows = ATT_GROUP * ATT_BLOCK
    qi = lax.broadcasted_iota(jnp.int32, (rows, 3 * ATT_BLOCK), 0) % ATT_BLOCK
    si = lax.broadcasted_iota(jnp.int32, (rows, 3 * ATT_BLOCK), 1)
    kblk = n - 1 + si // ATT_BLOCK
    mask = (jnp.abs(si - ATT_BLOCK - qi) <= ATT_BLOCK) & (kblk >= 0) & (kblk < nb) & is_lat
    nt = (((1,), (1,)), ((), ()))
    for g in range(ATT_KV_HEADS):
        ks = slice(g * HEAD_DIM, (g + 1) * HEAD_DIM)
        qs = jnp.concatenate(
            [q_ref[:, (g * ATT_GROUP + a) * HEAD_DIM:(g * ATT_GROUP + a + 1) * HEAD_DIM] for a in range(ATT_GROUP)],
            axis=0)
        kw = jnp.concatenate([k0_ref[:, ks], k1_ref[:, ks], k2_ref[:, ks]], axis=0)
        vw = jnp.concatenate([v0_ref[:, ks], v1_ref[:, ks], v2_ref[:, ks]], axis=0)
        s_win = jnp.where(mask, lax.dot_general(qs, kw, nt, preferred_element_type=F32), NEG_INF)
        s_ctx = lax.dot_general(qs, kc_ref[:, ks], nt, preferred_element_type=F32)
        row_head = lax.broadcasted_iota(jnp.int32, (rows, 1), 0) // ATT_BLOCK
        snk = jnp.zeros((rows, 1), F32)
        for a in range(ATT_GROUP):
            snk = jnp.where(row_head == a, sink_ref[g * ATT_GROUP + a], snk)
        mx = jnp.maximum(jnp.maximum(jnp.max(s_win, axis=-1, keepdims=True),
                                     jnp.max(s_ctx, axis=-1, keepdims=True)), snk)
        p_win = jnp.exp(s_win - mx)
        p_ctx = jnp.exp(s_ctx - mx)
        den = (jnp.sum(p_win, axis=-1, keepdims=True) + jnp.sum(p_ctx, axis=-1, keepdims=True)
               + jnp.exp(snk - mx))
        acc = (jnp.dot(p_win.astype(BF16), vw, preferred_element_type=F32)
               + jnp.dot(p_ctx.astype(BF16), vc_ref[:, ks], preferred_element_type=F32))
        out = acc / den
        for a in range(ATT_GROUP):
            h = g * ATT_GROUP + a
            o_ref[:, h * HEAD_DIM:(h + 1) * HEAD_DIM] = out[a * ATT_BLOCK:(a + 1) * ATT_BLOCK]


def _attention(qk, v, sink, batch, seq, ctx_len):
    m = qk.shape[0]
    nb = seq // ATT_BLOCK
    ncb = ctx_len // ATT_BLOCK
    ctx_blk0 = batch * nb
    kcol = ATT_WIDTH // KV_WIDTH

    def qrow(b, n):
        return jnp.where(n < nb, b * nb + n, ctx_blk0 + b * ncb + (n - nb))

    def krow(off):
        def f(b, n, *_):
            return (b * nb + jnp.clip(n + off, 0, nb - 1), kcol)
        return f

    def vrow(off):
        def f(b, n, *_):
            return (b * nb + jnp.clip(n + off, 0, nb - 1), 0)
        return f

    ctx_row = (batch * seq) // ctx_len
    grid_spec = pltpu.PrefetchScalarGridSpec(
        num_scalar_prefetch=1,
        grid=(batch, nb + ncb),
        in_specs=[pl.BlockSpec((ATT_BLOCK, ATT_WIDTH), lambda b, n, *_: (qrow(b, n), 0)),
                  pl.BlockSpec((ATT_BLOCK, KV_WIDTH), krow(-1)),
                  pl.BlockSpec((ATT_BLOCK, KV_WIDTH), krow(0)),
                  pl.BlockSpec((ATT_BLOCK, KV_WIDTH), krow(1)),
                  pl.BlockSpec((ATT_BLOCK, KV_WIDTH), vrow(-1)),
                  pl.BlockSpec((ATT_BLOCK, KV_WIDTH), vrow(0)),
                  pl.BlockSpec((ATT_BLOCK, KV_WIDTH), vrow(1)),
                  pl.BlockSpec((ctx_len, KV_WIDTH), lambda b, n, *_: (ctx_row + b, kcol)),
                  pl.BlockSpec((ctx_len, KV_WIDTH), lambda b, n, *_: (ctx_row + b, 0))],
        out_specs=pl.BlockSpec((ATT_BLOCK, ATT_WIDTH), lambda b, n, *_: (qrow(b, n), 0)),
    )
    return pl.pallas_call(
        functools.partial(_attn_body, nb=nb),
        grid_spec=grid_spec,
        out_shape=jax.ShapeDtypeStruct((m, ATT_WIDTH), F32),
        compiler_params=_cparams(("parallel", "arbitrary")),
        name="attention",
    )(sink, qk, qk, qk, qk, v, v, v, qk, v)


def _seq_edges(i, tm, n_lat, seq, ctx_len):
    r0 = i * tm
    in_lat = r0 < n_lat
    pos = jnp.where(in_lat, r0 % seq, (r0 - n_lat) % ctx_len)
    length = jnp.where(in_lat, seq, ctx_len)
    return pos == 0, pos + tm == length


def _halo_specs(tm, halo, width, n_rows):
    per = tm // halo
    last = n_rows // halo - 1
    prev = pl.BlockSpec((halo, width), lambda i: (jnp.maximum(i * per - 1, 0), 0))
    nxt = pl.BlockSpec((halo, width), lambda i: (jnp.minimum((i + 1) * per, last), 0))
    return prev, nxt


def _rwkvfeat_body(p_ref, pv_ref, nx_ref, mu_ref, w0_ref, wup_ref, a0_ref, aup_ref, gup_ref, kk_ref, ka_ref,
                   down_ref, up_ref, jv_ref, misc_ref, ext_ref, *, tm, n_lat, seq, ctx_len):
    first, last = _seq_edges(pl.program_id(0), tm, n_lat, seq, ctx_len)
    h = V7X_SUBLANES
    ext_ref[h:h + tm, :] = p_ref[...]
    ext_ref[0:h, :] = jnp.where(first, 0.0, pv_ref[...])
    ext_ref[h + tm:h + tm + h, :] = jnp.where(last, 0.0, nx_ref[...])
    p = p_ref[...]
    prev = ext_ref[h - 1:h - 1 + tm, :]
    nxt = ext_ref[h + 1:h + 1 + tm, :]
    xs = p + mu_ref[...] * (0.5 * (prev + nxt) - p)
    W = RWKV_WIDTH
    r = xs[:, 0:W]
    k = xs[:, W:2 * W]
    v = xs[:, 2 * W:3 * W]
    o = 3 * W
    wd = xs[:, o:o + N_DIR * W_LORA]
    ad = xs[:, o + N_DIR * W_LORA:o + N_DIR * (W_LORA + A_LORA)]
    gd = xs[:, o + N_DIR * (W_LORA + A_LORA):]
    w_lin = w0_ref[...] + jnp.dot(jnp.tanh(wd), wup_ref[...], preferred_element_type=F32, precision=HIGHEST)
    w_log = -jax.nn.softplus(-w_lin) - 0.5
    decay = jnp.exp(-jnp.exp(w_log))
    a = jax.nn.sigmoid(a0_ref[...] + jnp.dot(ad, aup_ref[...], preferred_element_type=F32, precision=HIGHEST))
    g = jnp.dot(jax.nn.sigmoid(gd), gup_ref[...], preferred_element_type=F32, precision=HIGHEST)
    kk = k * kk_ref[...]
    kk = kk * lax.rsqrt(_seg_reduce(kk * kk, down_ref[...], up_ref[...]) + 1e-12)
    for d in range(N_DIR):
        a_d = a[:, d * W:(d + 1) * W]
        base = d * N_JVEC * W
        jv_ref[:, base + 0 * W:base + 1 * W] = decay[:, d * W:(d + 1) * W]
        jv_ref[:, base + 1 * W:base + 2 * W] = -kk
        jv_ref[:, base + 2 * W:base + 3 * W] = kk * a_d
        jv_ref[:, base + 3 * W:base + 4 * W] = k * (1.0 + (a_d - 1.0) * ka_ref[...])
        jv_ref[:, base + 4 * W:base + 5 * W] = r
    misc_ref[:, 0:W] = r
    misc_ref[:, W:2 * W] = k
    misc_ref[:, 2 * W:3 * W] = v
    misc_ref[:, 3 * W:4 * W] = g


def _rwkvfeat(p_rw, mu, w0, wup2, a0, aup2, gup, k_k, k_a, tm, n_lat, seq, ctx_len):
    m = p_rw.shape[0]
    down, up = _seg_matrices(RWKV_WIDTH, mean=False)
    prev, nxt = _halo_specs(tm, V7X_SUBLANES, RWKV_COLS, m)
    full = lambda a: pl.BlockSpec(a.shape, lambda i: (0,) * a.ndim)
    params = [mu, w0, wup2, a0, aup2, gup, k_k, k_a, down, up]
    return pl.pallas_call(
        functools.partial(_rwkvfeat_body, tm=tm, n_lat=n_lat, seq=seq, ctx_len=ctx_len),
        grid=(m // tm,),
        in_specs=[pl.BlockSpec((tm, RWKV_COLS), lambda i: (i, 0)), prev, nxt] + [full(a) for a in params],
        out_specs=[pl.BlockSpec((tm, N_DIR * N_JVEC * RWKV_WIDTH), lambda i: (i, 0)),
                   pl.BlockSpec((tm, 4 * RWKV_WIDTH), lambda i: (i, 0))],
        out_shape=[jax.ShapeDtypeStruct((m, N_DIR * N_JVEC * RWKV_WIDTH), F32),
                   jax.ShapeDtypeStruct((m, 4 * RWKV_WIDTH), F32)],
        scratch_shapes=[pltpu.VMEM((tm + 2 * V7X_SUBLANES, RWKV_COLS), F32)],
        compiler_params=_cparams(("parallel",)),
        name="rwkvfeat",
    )(p_rw, p_rw, p_rw, *params)


def _scan_body(*refs, ts, n_chain_grp):
    nq = n_chain_grp
    rev_ref = refs[0]
    j_refs = refs[1:1 + nq]
    v_refs = refs[1 + nq:1 + 2 * nq]
    y_refs = refs[1 + 2 * nq:1 + 2 * nq + N_DIR]
    s_ref, jt_ref, vs_ref, ys_ref, xt_ref = refs[1 + 2 * nq + N_DIR:]
    W = RWKV_WIDTH
    rep = V7X_LANES // (nq * RWKV_HEADS)

    @pl.when(pl.program_id(0) == 0)
    def _():
        s_ref[...] = jnp.zeros_like(s_ref)

    def flip(x):
        return jnp.dot(rev_ref[...], x, preferred_element_type=F32, precision=HIGHEST)

    def backward(q):
        return q >= nq // N_DIR

    for vec in range(N_JVEC):
        for q in range(nq):
            x = j_refs[q][:, vec * W:(vec + 1) * W]
            xt_ref[q] = (flip(x) if backward(q) else x).T

        def jtile(j, carry, vec=vec):
            pieces = []
            for q in range(nq):
                pieces += [xt_ref[q, pl.ds(j, RWKV_HEADS, stride=HEAD_DIM), :]] * rep
            jt_ref[vec, j] = jnp.concatenate(pieces, axis=0).T
            return carry

        lax.fori_loop(0, HEAD_DIM, jtile, 0, unroll=8)

    for q in range(nq):
        x = v_refs[q][...]
        xt_ref[q] = (flip(x) if backward(q) else x).T
    for ihi in range(SCAN_IHI):
        pieces = []
        for q in range(nq):
            for ilo in range(SCAN_ILO):
                pieces.append(xt_ref[q, pl.ds(ihi * SCAN_ILO + ilo, RWKV_HEADS, stride=HEAD_DIM), :])
        vs_ref[pl.ds(ihi, ts, stride=SCAN_IHI), :] = jnp.concatenate(pieces, axis=0).T

    n_acc = 8

    def tree_sum(parts):
        while len(parts) > 1:
            parts = [parts[k] + parts[k + 1] for k in range(0, len(parts), 2)]
        return parts[0]

    def add_part(parts, k, p):
        parts[k % n_acc] = p if parts[k % n_acc] is None else parts[k % n_acc] + p

    def step(t, sa):
        row = pl.ds(t, 1)
        nxt = pl.ds(jnp.minimum(t + 1, ts - 1), 1)
        base = pl.multiple_of(t * SCAN_IHI, SCAN_IHI)
        v = vs_ref[pl.ds(base, SCAN_IHI), :]
        yparts = [None] * n_acc
        sparts = [None] * n_acc
        for j in range(HEAD_DIM):
            sj = s_ref[j] * jt_ref[0, j, row, :] + sa * jt_ref[2, j, row, :] + v * jt_ref[3, j, row, :]
            s_ref[j] = sj
            add_part(yparts, j, sj * jt_ref[4, j, row, :])
            add_part(sparts, j, sj * jt_ref[1, j, nxt, :])
        ys_ref[pl.ds(base, SCAN_IHI), :] = tree_sum(yparts)
        return tree_sum(sparts)

    first = [None] * n_acc
    for j in range(HEAD_DIM):
        add_part(first, j, s_ref[j] * jt_ref[1, j, pl.ds(0, 1), :])
    lax.fori_loop(0, ts, step, tree_sum(first))

    for ihi in range(SCAN_IHI):
        yt = ys_ref[pl.ds(ihi, ts, stride=SCAN_IHI), :].T
        for q in range(nq):
            for ilo in range(SCAN_ILO):
                r0 = (q * SCAN_ILO + ilo) * RWKV_HEADS
                xt_ref[q, pl.ds(ihi * SCAN_ILO + ilo, RWKV_HEADS, stride=HEAD_DIM), :] = yt[r0:r0 + RWKV_HEADS]
    per_dir = nq // N_DIR
    for q in range(nq):
        y = xt_ref[q].T
        y_refs[q // per_dir][q % per_dir] = flip(y) if backward(q) else y


def _rwkv_scan(jv, misc, batch, seq, ctx_len, ts):
    nq = N_DIR * batch
    assert nq * RWKV_HEADS * SCAN_ILO == V7X_LANES and seq % ts == 0 and ctx_len % ts == 0
    n_l, n_c = seq // ts, ctx_len // ts
    W = RWKV_WIDTH

    def row_block(d, b):
        def f(i):
            is_ctx = i < n_c
            k_ctx = (n_c - 1 - i) if d else i
            k_lat = (n_l - 1 - (i - n_c)) if d else (i - n_c)
            return jnp.where(is_ctx, batch * n_l + b * n_c + k_ctx, b * n_l + k_lat)
        return f

    def pos_block(d):
        def f(i):
            return jnp.where(i < n_c, n_c - 1 - i, n_c + n_l - 1 - (i - n_c)) if d else i
        return f

    chains = [(d, b) for d in range(N_DIR) for b in range(batch)]
    j_specs = [pl.BlockSpec((ts, N_JVEC * W), lambda i, d=d, f=row_block(d, b): (f(i), d)) for d, b in chains]
    v_specs = [pl.BlockSpec((ts, W), lambda i, f=row_block(d, b): (f(i), 2)) for d, b in chains]
    y_specs = [pl.BlockSpec((batch, ts, W), lambda i, f=pos_block(d): (0, f(i), 0)) for d in range(N_DIR)]
    rev = jnp.asarray(np.eye(ts, dtype=np.float32)[::-1].copy())
    return pl.pallas_call(
        functools.partial(_scan_body, ts=ts, n_chain_grp=nq),
        grid=(n_c + n_l,),
        in_specs=[pl.BlockSpec((ts, ts), lambda i: (0, 0))] + j_specs + v_specs,
        out_specs=y_specs,
        out_shape=[jax.ShapeDtypeStruct((batch, seq + ctx_len, W), F32)] * N_DIR,
        scratch_shapes=[pltpu.VMEM((HEAD_DIM, SCAN_IHI, V7X_LANES), F32),
                        pltpu.VMEM((N_JVEC, HEAD_DIM, ts, V7X_LANES), F32),
                        pltpu.VMEM((ts * SCAN_IHI, V7X_LANES), F32),
                        pltpu.VMEM((ts * SCAN_IHI, V7X_LANES), F32),
                        pltpu.VMEM((nq, W, ts), F32)],
        compiler_params=_cparams(("arbitrary",)),
        name="scan",
    )(rev, *([jv] * nq), *([misc] * nq))


def _rwkvout_body(yf_ref, yb_ref, misc_ref, rk_ref, lg_ref, lb_ref, dmean_ref, dsum_ref, up_ref, o_ref):
    W = RWKV_WIDTH
    y = yf_ref[0] + yb_ref[0]
    r = misc_ref[:, 0:W]
    k = misc_ref[:, W:2 * W]
    v = misc_ref[:, 2 * W:3 * W]
    g = misc_ref[:, 3 * W:4 * W]
    mean = _seg_reduce(y, dmean_ref[...], up_ref[...])
    yc = y - mean
    var = _seg_reduce(yc * yc, dmean_ref[...], up_ref[...])
    yn = yc * lax.rsqrt(var + GN_EPS) * lg_ref[...] + lb_ref[...]
    bonus = _seg_reduce(r * k * rk_ref[...], dsum_ref[...], up_ref[...]) * v
    o_ref[...] = (yn + bonus) * g


def _rwkvout(y2, misc, r_k, ln_g, ln_b, tm, n_lat, seq, ctx_len):
    m = misc.shape[0]

    def ypos(i):
        r0 = i * tm
        in_lat = r0 < n_lat
        b = jnp.where(in_lat, r0 // seq, (r0 - n_lat) // ctx_len)
        pos = jnp.where(in_lat, ctx_len + r0 % seq, (r0 - n_lat) % ctx_len)
        return (b, pos // tm, 0)

    dmean, up = _seg_matrices(RWKV_WIDTH, mean=True)
    dsum, _ = _seg_matrices(RWKV_WIDTH, mean=False)
    full = lambda a: pl.BlockSpec(a.shape, lambda i: (0,) * a.ndim)
    params = [r_k, ln_g, ln_b, dmean, dsum, up]
    return pl.pallas_call(
        _rwkvout_body,
        grid=(m // tm,),
        in_specs=[pl.BlockSpec((1, tm, RWKV_WIDTH), ypos), pl.BlockSpec((1, tm, RWKV_WIDTH), ypos),
                  pl.BlockSpec((tm, 4 * RWKV_WIDTH), lambda i: (i, 0))] + [full(a) for a in params],
        out_specs=pl.BlockSpec((tm, RWKV_WIDTH), lambda i: (i, 0)),
        out_shape=jax.ShapeDtypeStruct((m, RWKV_WIDTH), F32),
        compiler_params=_cparams(("parallel",)),
        name="rwkvout",
    )(y2[0], y2[1], misc, *params)


def _conv_body(p_ref, pv_ref, nx_ref, w_ref, b_ref, g_ref, o_ref, ext_ref, *, tm, n_lat, seq, ctx_len):
    first, last = _seq_edges(pl.program_id(0), tm, n_lat, seq, ctx_len)
    C = CONV_WIDTH
    H = CONV_HALO

    def glu(t):
        return t[:, :C] * jax.nn.sigmoid(t[:, C:])

    ext_ref[H:H + tm, :] = glu(p_ref[...])
    ext_ref[0:H, :] = jnp.where(first, 0.0, glu(pv_ref[...]))
    ext_ref[H + tm:H + tm + H, :] = jnp.where(last, 0.0, glu(nx_ref[...]))
    rows = 64
    for c in range(tm // rows):
        acc = jnp.zeros((rows, C), F32)
        for t in range(CONV_KSIZE):
            off = c * rows + H - CONV_PAD + t
            acc = acc + ext_ref[off:off + rows, :] * w_ref[t:t + 1, :]
        u = acc + b_ref[...]
        ms = jnp.mean(u * u, axis=-1, keepdims=True)
        z = u * lax.rsqrt(ms + EPS) * g_ref[...]
        o_ref[c * rows:(c + 1) * rows, :] = z * jax.nn.sigmoid(z)


def _conv(p_cv, w, b, g, tm, n_lat, seq, ctx_len):
    m = p_cv.shape[0]
    prev, nxt = _halo_specs(tm, CONV_HALO, 2 * CONV_WIDTH, m)
    full = lambda a: pl.BlockSpec(a.shape, lambda i: (0,) * a.ndim)
    return pl.pallas_call(
        functools.partial(_conv_body, tm=tm, n_lat=n_lat, seq=seq, ctx_len=ctx_len),
        grid=(m // tm,),
        in_specs=[pl.BlockSpec((tm, 2 * CONV_WIDTH), lambda i: (i, 0)), prev, nxt, full(w), full(b), full(g)],
        out_specs=pl.BlockSpec((tm, CONV_WIDTH), lambda i: (i, 0)),
        out_shape=jax.ShapeDtypeStruct((m, CONV_WIDTH), F32),
        scratch_shapes=[pltpu.VMEM((tm + 2 * CONV_HALO, CONV_WIDTH), F32)],
        compiler_params=_cparams(("parallel",)),
        name="conv",
    )(p_cv, p_cv, p_cv, w, b, g)


def _outproj_body(ya_ref, yr_ref, yc_ref, x_ref, g_ref, wa_ref, wr_ref, wc_ref, o_ref):
    acc = jnp.dot(ya_ref[...].astype(BF16), wa_ref[...], preferred_element_type=F32)
    acc += jnp.dot(yr_ref[...].astype(BF16), wr_ref[...], preferred_element_type=F32)
    acc += jnp.dot(yc_ref[...].astype(BF16), wc_ref[...], preferred_element_type=F32)
    o_ref[...] = x_ref[...] + g_ref[0] * acc


def _outproj(ya, yr, yc, x, gate, w, n_lat, seq, tm, tn):
    m, d = x.shape
    n_grp = gate.shape[0] - 1

    def grp(i, j):
        r0 = i * tm
        return (jnp.where(r0 < n_lat, r0 // seq, n_grp), 0, j)

    wa, wr, wc = w[:ATT_WIDTH], w[ATT_WIDTH:ATT_WIDTH + RWKV_WIDTH], w[ATT_WIDTH + RWKV_WIDTH:]
    return pl.pallas_call(
        _outproj_body,
        grid=(m // tm, d // tn),
        in_specs=[pl.BlockSpec((tm, ATT_WIDTH), lambda i, j: (i, 0)),
                  pl.BlockSpec((tm, RWKV_WIDTH), lambda i, j: (i, 0)),
                  pl.BlockSpec((tm, CONV_WIDTH), lambda i, j: (i, 0)),
                  pl.BlockSpec((tm, tn), lambda i, j: (i, j)),
                  pl.BlockSpec((1, 1, tn), grp),
                  pl.BlockSpec((ATT_WIDTH, tn), lambda i, j: (0, j)),
                  pl.BlockSpec((RWKV_WIDTH, tn), lambda i, j: (0, j)),
                  pl.BlockSpec((CONV_WIDTH, tn), lambda i, j: (0, j))],
        out_specs=pl.BlockSpec((tm, tn), lambda i, j: (i, j)),
        out_shape=jax.ShapeDtypeStruct((m, d), F32),
        compiler_params=_cparams(("parallel", "arbitrary")),
        name="outproj",
    )(ya, yr, yc, x, gate, wa, wr, wc)


def _peer_pairs():
    n = PEER_TOPK + 1
    return [(a, b) for a in range(n) for b in range(n) if (a + 1) * (b + 1) <= n]


def _top_distinct(s, n):
    vals, cnts = [], []
    v = s
    for _ in range(n):
        mx = jnp.max(v, axis=0, keepdims=True)
        hit = v == mx
        cnt = jnp.sum(hit.astype(F32), axis=0, keepdims=True)
        vals.append(mx)
        cnts.append(jnp.where(mx == -jnp.inf, 0.0, cnt))
        v = jnp.where(hit, -jnp.inf, v)
    return vals, cnts


def _peertopk_body(q_ref, keys_ref, e2_ref, a1_ref, e1_ref):
    n = PEER_TOPK + 1
    nt = (((1,), (1,)), ((), ()))
    pairs = _peer_pairs()
    for h in range(PEER_HEADS):
        sc = []
        for p in range(2):
            c0 = (h * 2 + p) * PEER_N_KEYS
            sc.append(lax.dot_general(keys_ref[h, p], q_ref[:, c0:c0 + PEER_N_KEYS], nt,
                                      preferred_element_type=F32, precision=HIGHEST))
        (v1, n1), (v2, n2) = _top_distinct(sc[0], n), _top_distinct(sc[1], n)
        cv = jnp.concatenate([v1[a] + v2[b] for a, b in pairs], axis=0)
        cm = jnp.concatenate([n1[a] * n2[b] for a, b in pairs], axis=0)
        top = v1[0] + v2[0]
        cum = jnp.zeros_like(top)
        t16 = jnp.full_like(top, -jnp.inf)
        t17 = jnp.full_like(top, -jnp.inf)
        z = jnp.zeros_like(top)
        for _ in range(n):
            mx = jnp.max(cv, axis=0, keepdims=True)
            hit = cv == mx
            cnt = jnp.where(mx == -jnp.inf, 0.0, jnp.sum(jnp.where(hit, cm, 0.0), axis=0, keepdims=True))
            new = cum + cnt
            t16 = jnp.where((cum < PEER_TOPK) & (new >= PEER_TOPK), mx, t16)
            t17 = jnp.where((cum < n) & (new >= n), mx, t17)
            take = jnp.minimum(cnt, jnp.maximum(PEER_TOPK - cum, 0.0))
            z = z + jnp.where(take > 0, take * jnp.exp(mx - top), 0.0)
            cum = new
            cv = jnp.where(hit, -jnp.inf, cv)
        thr = 0.5 * (t16 + t17)
        e2_ref[h] = jnp.exp(sc[1] - v2[0])
        a1_ref[h] = jnp.exp(thr - sc[0] - v2[0])
        e1_ref[h] = jnp.exp(sc[0] - v1[0]) / z


def _peertopk(q, keys, tt):
    m, qw = q.shape
    shp = jax.ShapeDtypeStruct((PEER_HEADS, PEER_N_KEYS, m), F32)
    ospec = pl.BlockSpec((PEER_HEADS, PEER_N_KEYS, tt), lambda i: (0, 0, i))
    return pl.pallas_call(
        _peertopk_body,
        grid=(m // tt,),
        in_specs=[pl.BlockSpec((tt, qw), lambda i: (i, 0)),
                  pl.BlockSpec(keys.shape, lambda i: (0, 0, 0, 0))],
        out_specs=[ospec] * 3,
        out_shape=[shp] * 3,
        compiler_params=_cparams(("parallel",)),
        name="peertopk",
    )(q, keys)


def _peerdense_body(h_ref, u_ref, vt_ref, e2_ref, a1_ref, e1_ref, x_ref, g_ref, o_ref, acc_ref, a_ref, ht_ref,
                    st_ref, *, ec):
    e = pl.program_id(1)
    tt = ht_ref.shape[1]

    @pl.when(e == 0)
    def _():
        acc_ref[...] = jnp.zeros_like(acc_ref)
        ht_ref[...] = h_ref[...].astype(F32).T.astype(BF16)

    st_ref[...] = jnp.dot(u_ref[...], ht_ref[...], preferred_element_type=F32)
    per = ec // PEER_N_KEYS
    tc = 2 * V7X_LANES
    rh = PEER_N_KEYS // 2
    for c in range(tt // tc):
        cs = pl.ds(c * tc, tc)
        for ii in range(per):
            i1 = e * per + ii
            for r in range(PEER_N_KEYS // rh):
                ks = pl.ds(r * rh, rh)
                w = None
                for h in range(PEER_HEADS):
                    e2 = e2_ref[h, ks, cs]
                    t = jnp.where(e2 >= a1_ref[h, pl.ds(i1, 1), cs], e2, 0.0) * e1_ref[h, pl.ds(i1, 1), cs]
                    w = t if w is None else w + t
                rows = pl.ds(ii * PEER_N_KEYS + r * rh, rh)
                s = st_ref[rows, cs]
                act = 0.5 * s * (1.0 + lax.erf(s * np.float32(1.0 / np.sqrt(2.0))))
                a_ref[rows, cs] = (act * w).astype(BF16)
    acc_ref[...] += jnp.dot(vt_ref[...], a_ref[...], preferred_element_type=F32)

    @pl.when(e == pl.num_programs(1) - 1)
    def _():
        o_ref[...] = x_ref[...] + g_ref[0] * acc_ref[...].T


def _peerdense(h2, u, vt, e2, a1, e1, x, gate, n_lat, seq, tt, ec, m_out):
    d = x.shape[1]
    m = m_out
    n_chunks = u.shape[0] // ec
    n_grp = gate.shape[0] - 1

    def grp(i, e):
        r0 = i * tt
        return (jnp.where(r0 < n_lat, r0 // seq, n_grp), 0, 0)

    once = pl.Buffered(1)
    tok = pl.BlockSpec((PEER_HEADS, PEER_N_KEYS, tt), lambda i, e: (0, 0, i), pipeline_mode=once)
    return pl.pallas_call(
        functools.partial(_peerdense_body, ec=ec),
        grid=(m // tt, n_chunks),
        in_specs=[pl.BlockSpec((tt, d), lambda i, e: (i, 0), pipeline_mode=once),
                  pl.BlockSpec((ec, d), lambda i, e: (e, 0)),
                  pl.BlockSpec((d, ec), lambda i, e: (0, e)),
                  tok, tok, tok,
                  pl.BlockSpec((tt, d), lambda i, e: (i, 0), pipeline_mode=once),
                  pl.BlockSpec((1, 1, d), grp)],
        out_specs=pl.BlockSpec((tt, d), lambda i, e: (i, 0)),
        out_shape=jax.ShapeDtypeStruct((m, d), F32),
        scratch_shapes=[pltpu.VMEM((d, tt), F32), pltpu.VMEM((ec, tt), BF16), pltpu.VMEM((d, tt), BF16),
                        pltpu.VMEM((ec, tt), F32)],
        compiler_params=_cparams(("parallel", "arbitrary")),
        name="peerdense",
    )(h2, u, vt, e2, a1, e1, x, gate)


def _tiles(seq, ctx_len, batch):
    ctx_rows = batch * ctx_len
    return dict(
        mm=min(512, seq, ctx_rows),
        seqt=min(256, seq, ctx_len),
        peer=min(512, seq, ctx_rows),
        scan=min(128, seq, ctx_len),
    )


def _rope_tables(batch, seq, ctx_len):
    inv = ROPE_THETA ** (-jnp.arange(0, ROPE_AXIS_DIM, 2, dtype=F32) / ROPE_AXIS_DIM)
    t = jnp.arange(seq, dtype=jnp.int32)
    ang_r = (t // GRID_W).astype(F32)[:, None] * inv[None, :]
    ang_c = (t % GRID_W).astype(F32)[:, None] * inv[None, :]
    cos = jnp.concatenate([jnp.cos(ang_r)] * 2 + [jnp.cos(ang_c)] * 2, axis=-1)
    sin = jnp.concatenate([-jnp.sin(ang_r), jnp.sin(ang_r), -jnp.sin(ang_c), jnp.sin(ang_c)], axis=-1)
    n_ctx = batch * ctx_len
    cos = jnp.concatenate([jnp.tile(cos, (batch, 1)), jnp.ones((n_ctx, HEAD_DIM), F32)], axis=0)
    sin = jnp.concatenate([jnp.tile(sin, (batch, 1)), jnp.zeros((n_ctx, HEAD_DIM), F32)], axis=0)
    return jnp.tile(cos, (1, 2)), jnp.tile(sin, (1, 2))


def _block_diag2(w):
    z = jnp.zeros_like(w[0])
    return jnp.concatenate([jnp.concatenate([w[0], z], axis=1), jnp.concatenate([z, w[1]], axis=1)], axis=0)


def kernel(x, c, ctx, c_ctx, norm1_g, norm2_g, w_ada, b_ada, w_in, w_out, q_norm_g, k_norm_g, attn_sink, rwkv_mu, rwkv_w0, rwkv_w_up, rwkv_a0, rwkv_a_up, rwkv_g_up, rwkv_k_k, rwkv_k_a, rwkv_r_k, rwkv_ln_g, rwkv_ln_b, conv_w, conv_b, conv_norm_g, peer_wq, peer_keys, peer_u, peer_v):
    batch, seq, d = x.shape
    ctx_len = ctx.shape[1]
    depth = w_in.shape[0]
    n_lat = batch * seq
    tl = _tiles(seq, ctx_len, batch)
    assert seq % ATT_BLOCK == 0 and ctx_len % ATT_BLOCK == 0 and n_lat % ctx_len == 0
    assert (seq + ctx_len) % tl["scan"] == 0

    xt = jnp.concatenate([x.reshape(n_lat, d), ctx.reshape(batch * ctx_len, d)], axis=0)
    cc = jnp.concatenate([c, c_ctx[None, :], jnp.zeros((8 - batch - 1, d), F32)], axis=0)
    cos2, sin2 = _rope_tables(batch, seq, ctx_len)
    row = lambda a: a.reshape(1, -1)

    for i in range(depth):
        mod = _ada(cc, w_ada[i], row(b_ada[i]))[:batch + 1].reshape(batch + 1, 6, 1, d)
        sh1, sc1, g1, sh2, sc2, g2 = (mod[:, k] for k in range(6))

        w_in_b = w_in[i].astype(BF16)
        p_att, h1 = _modmm(xt, row(norm1_g[i]), sc1, sh1, w_in_b[:, :ATT_COLS], n_lat, seq, tl["mm"], 768)
        p_rw = _mm(h1, w_in_b[:, ATT_COLS:ATT_COLS + RWKV_COLS], tl["mm"], 640)
        p_cv = _mm(h1, w_in_b[:, ATT_COLS + RWKV_COLS:], tl["mm"], 1024)

        gain = jnp.concatenate([jnp.tile(q_norm_g[i] * HEAD_DIM ** -0.5, ATT_HEADS), jnp.tile(k_norm_g[i], ATT_KV_HEADS)])
        qk, vv = _qkprep(p_att, row(gain), cos2, sin2, tl["seqt"])
        y_att = _attention(qk, vv, attn_sink[i], batch, seq, ctx_len)

        jv, misc = _rwkvfeat(p_rw, row(rwkv_mu[i]), row(rwkv_w0[i]), _block_diag2(rwkv_w_up[i]),
                             row(rwkv_a0[i]), _block_diag2(rwkv_a_up[i]), rwkv_g_up[i],
                             row(rwkv_k_k[i]), row(rwkv_k_a[i]), tl["seqt"], n_lat, seq, ctx_len)
        y2 = _rwkv_scan(jv, misc, batch, seq, ctx_len, tl["scan"])
        y_rw = _rwkvout(y2, misc, row(rwkv_r_k[i]), row(rwkv_ln_g[i]), row(rwkv_ln_b[i]), tl["seqt"],
                        n_lat, seq, ctx_len)

        y_cv = _conv(p_cv, conv_w[i], row(conv_b[i]), row(conv_norm_g[i]), tl["seqt"], n_lat, seq, ctx_len)

        xt = _outproj(y_att, y_rw, y_cv, xt, g1, w_out[i].astype(BF16), n_lat, seq, tl["mm"], 1024)

        q, h2 = _modmm(xt, row(norm2_g[i]), sc2, sh2, peer_wq[i].astype(BF16), n_lat, seq, tl["mm"], 1024)
        e2, a1, e1 = _peertopk(q, peer_keys[i], tl["peer"])
        xt = _peerdense(h2, peer_u[i].astype(BF16), peer_v[i].T.astype(BF16), e2, a1, e1, xt, g2,
                        n_lat, seq, tl["peer"], 1024, n_lat if i == depth - 1 else xt.shape[0])
    return xt.reshape(batch, seq, d)
```

```python
import functools

import jax
import jax.numpy as jnp
import numpy as np
from jax import lax
from jax.experimental import pallas as pl
from jax.experimental.pallas import tpu as pltpu

F32 = jnp.float32
BF16 = jnp.bfloat16
HIGHEST = lax.Precision.HIGHEST

V7X_LANES = 128
V7X_SUBLANES = 8
V7X_VMEM_BYTES = 64 * 1024 * 1024
VMEM_LIMIT = V7X_VMEM_BYTES - 8 * 1024 * 1024

HEAD_DIM = 64
EPS = 1e-6
GN_EPS = 64e-5
NEG_INF = -1e30
GRID_W = 64
ATT_HEADS = 16
ATT_KV_HEADS = 4
ATT_GROUP = ATT_HEADS // ATT_KV_HEADS
ATT_WIDTH = ATT_HEADS * HEAD_DIM
KV_WIDTH = ATT_KV_HEADS * HEAD_DIM
QK_WIDTH = ATT_WIDTH + KV_WIDTH
ATT_COLS = ATT_WIDTH + 2 * KV_WIDTH
ATT_BLOCK = 128
ROPE_THETA = 10000.0
ROPE_AXIS_DIM = HEAD_DIM // 2
ROPE_HALF = ROPE_AXIS_DIM // 2
RWKV_HEADS = 8
RWKV_WIDTH = RWKV_HEADS * HEAD_DIM
W_LORA = 64
A_LORA = 64
G_LORA = 128
N_DIR = 2
RWKV_COLS = 3 * RWKV_WIDTH + N_DIR * W_LORA + N_DIR * A_LORA + G_LORA
CONV_WIDTH = 512
CONV_KSIZE = 31
CONV_PAD = (CONV_KSIZE - 1) // 2
CONV_HALO = 16
PEER_HEADS = 8
PEER_N_KEYS = 128
PEER_TOPK = 16
N_JVEC = 5
SCAN_ILO = 4
SCAN_IHI = HEAD_DIM // SCAN_ILO


def _cparams(sem):
    return pltpu.CompilerParams(dimension_semantics=sem, vmem_limit_bytes=VMEM_LIMIT)


def _seg_matrices(width, mean):
    nseg = width // HEAD_DIM
    seg = np.arange(width) // HEAD_DIM
    down = np.zeros((width, V7X_LANES), np.float32)
    down[np.arange(width), seg] = 1.0 / HEAD_DIM if mean else 1.0
    up = np.zeros((V7X_LANES, width), np.float32)
    up[seg, np.arange(width)] = 1.0
    assert nseg <= V7X_LANES
    return jnp.asarray(down, BF16), jnp.asarray(up, BF16)


def _split2(a):
    hi = a.astype(BF16)
    return hi, (a - hi.astype(F32)).astype(BF16)


def _dot_exact_rhs(a, m):
    hi, lo = _split2(a)
    return jnp.dot(hi, m, preferred_element_type=F32) + jnp.dot(lo, m, preferred_element_type=F32)


def _dot3(a, b):
    a_hi, a_lo = _split2(a)
    b_hi, b_lo = _split2(b)
    return (jnp.dot(a_hi, b_hi, preferred_element_type=F32) + jnp.dot(a_lo, b_hi, preferred_element_type=F32)
            + jnp.dot(a_hi, b_lo, preferred_element_type=F32))


def _seg_reduce(x, down, up):
    return _dot_exact_rhs(_dot_exact_rhs(x, down), up)


def _ada_body(c_ref, w_ref, b_ref, o_ref):
    cc = c_ref[...]
    s = (cc * jax.nn.sigmoid(cc)).astype(BF16)
    o_ref[...] = jnp.dot(s, w_ref[...].astype(BF16), preferred_element_type=F32) + b_ref[...]


def _ada(cc, w, b):
    d, n = w.shape
    tn = 1024
    return pl.pallas_call(
        _ada_body,
        grid=(n // tn,),
        in_specs=[pl.BlockSpec((8, d), lambda j: (0, 0)),
                  pl.BlockSpec((d, tn), lambda j: (0, j)),
                  pl.BlockSpec((1, tn), lambda j: (0, j))],
        out_specs=pl.BlockSpec((8, tn), lambda j: (0, j)),
        out_shape=jax.ShapeDtypeStruct((8, n), F32),
        compiler_params=_cparams(("arbitrary",)),
        name="ada",
    )(cc, w, b)


def _modmm_body(x_ref, g_ref, sc_ref, sh_ref, w_ref, o_ref, h_ref):
    @pl.when(pl.program_id(1) == 0)
    def _():
        x = x_ref[...]
        ms = jnp.mean(x * x, axis=-1, keepdims=True)
        y = x * lax.rsqrt(ms + EPS) * g_ref[...]
        h_ref[...] = (y * (1.0 + sc_ref[0]) + sh_ref[0]).astype(BF16)

    o_ref[...] = jnp.dot(h_ref[...], w_ref[...], preferred_element_type=F32)


def _modmm(x, g, sc, sh, w, n_lat, seq, tm, tn):
    m, d = x.shape
    n = w.shape[1]
    n_grp = sc.shape[0] - 1

    def grp(i, j):
        r0 = i * tm
        return (jnp.where(r0 < n_lat, r0 // seq, n_grp), 0, 0)

    return pl.pallas_call(
        _modmm_body,
        grid=(m // tm, n // tn),
        in_specs=[pl.BlockSpec((tm, d), lambda i, j: (i, 0)),
                  pl.BlockSpec((1, d), lambda i, j: (0, 0)),
                  pl.BlockSpec((1, 1, d), grp),
                  pl.BlockSpec((1, 1, d), grp),
                  pl.BlockSpec((d, tn), lambda i, j: (0, j))],
        out_specs=[pl.BlockSpec((tm, tn), lambda i, j: (i, j)),
                   pl.BlockSpec((tm, d), lambda i, j: (i, 0))],
        out_shape=[jax.ShapeDtypeStruct((m, n), F32), jax.ShapeDtypeStruct((m, d), BF16)],
        compiler_params=_cparams(("parallel", "arbitrary")),
        name="modmm",
    )(x, g, sc, sh, w)


def _mm_body(h_ref, w_ref, o_ref):
    o_ref[...] = jnp.dot(h_ref[...], w_ref[...], preferred_element_type=F32)


def _mm(h, w, tm, tn):
    m, d = h.shape
    n = w.shape[1]
    return pl.pallas_call(
        _mm_body,
        grid=(m // tm, n // tn),
        in_specs=[pl.BlockSpec((tm, d), lambda i, j: (i, 0)),
                  pl.BlockSpec((d, tn), lambda i, j: (0, j))],
        out_specs=pl.BlockSpec((tm, tn), lambda i, j: (i, j)),
        out_shape=jax.ShapeDtypeStruct((m, n), F32),
        compiler_params=_cparams(("parallel", "arbitrary")),
        name="mm",
    )(h, w)


def _qkprep_body(p_ref, gain_ref, cos_ref, sin_ref, down_ref, up_ref, qk_ref, v_ref):
    x = p_ref[:, :QK_WIDTH]
    ms = _seg_reduce(x * x, down_ref[...], up_ref[...])
    y = x * lax.rsqrt(ms + EPS) * gain_ref[...]
    cos = cos_ref[...]
    sin = sin_ref[...]
    lane = lax.broadcasted_iota(jnp.int32, cos.shape, 1)
    first = (lane % ROPE_AXIS_DIM) < ROPE_HALF
    for blk in range(QK_WIDTH // V7X_LANES):
        yb = y[:, blk * V7X_LANES:(blk + 1) * V7X_LANES]
        partner = jnp.where(first, pltpu.roll(yb, V7X_LANES - ROPE_HALF, 1), pltpu.roll(yb, ROPE_HALF, 1))
        qk_ref[:, blk * V7X_LANES:(blk + 1) * V7X_LANES] = (yb * cos + partner * sin).astype(BF16)
    v_ref[...] = p_ref[:, QK_WIDTH:].astype(BF16)


def _qkprep(p_att, gain, cos2, sin2, tm):
    m = p_att.shape[0]
    down, up = _seg_matrices(QK_WIDTH, mean=True)
    return pl.pallas_call(
        _qkprep_body,
        grid=(m // tm,),
        in_specs=[pl.BlockSpec((tm, ATT_COLS), lambda i: (i, 0)),
                  pl.BlockSpec((1, QK_WIDTH), lambda i: (0, 0)),
                  pl.BlockSpec((tm, V7X_LANES), lambda i: (i, 0)),
                  pl.BlockSpec((tm, V7X_LANES), lambda i: (i, 0)),
                  pl.BlockSpec(down.shape, lambda i: (0, 0)),
                  pl.BlockSpec(up.shape, lambda i: (0, 0))],
        out_specs=[pl.BlockSpec((tm, QK_WIDTH), lambda i: (i, 0)),
                   pl.BlockSpec((tm, KV_WIDTH), lambda i: (i, 0))],
        out_shape=[jax.ShapeDtypeStruct((m, QK_WIDTH), BF16), jax.ShapeDtypeStruct((m, KV_WIDTH), BF16)],
        compiler_params=_cparams(("parallel",)),
        name="qkprep",
    )(p_att, gain, cos2, sin2, down, up)


def _attn_body(sink_ref, q_ref, k0_ref, k1_ref, k2_ref, v0_ref, v1_ref, v2_ref, kc_ref, vc_ref, o_ref, *, nb):
    n = pl.program_id(1)
    is_lat = n < nb
    rows = ATT_GROUP * ATT_BLOCK
    qi = lax.broadcasted_iota(jnp.int32, (rows, 3 * ATT_BLOCK), 0) % ATT_BLOCK
    si = lax.broadcasted_iota(jnp.int32, (rows, 3 * ATT_BLOCK), 1)
    kblk = n - 1 + si // ATT_BLOCK
    mask = (jnp.abs(si - ATT_BLOCK - qi) <= ATT_BLOCK) & (kblk >= 0) & (kblk < nb) & is_lat
    nt = (((1,), (1,)), ((), ()))
    for g in range(ATT_KV_HEADS):
        ks = slice(g * HEAD_DIM, (g + 1) * HEAD_DIM)
        qs = jnp.concatenate(
            [q_ref[:, (g * ATT_GROUP + a) * HEAD_DIM:(g * ATT_GROUP + a + 1) * HEAD_DIM] for a in range(ATT_GROUP)],
            axis=0)
        kw = jnp.concatenate([k0_ref[:, ks], k1_ref[:, ks], k2_ref[:, ks]], axis=0)
        vw = jnp.concatenate([v0_ref[:, ks], v1_ref[:, ks], v2_ref[:, ks]], axis=0)
        s_win = jnp.where(mask, lax.dot_general(qs, kw, nt, preferred_element_type=F32), NEG_INF)
        s_ctx = lax.dot_general(qs, kc_ref[:, ks], nt, preferred_element_type=F32)
        row_head = lax.broadcasted_iota(jnp.int32, (rows, 1), 0) // ATT_BLOCK
        snk = jnp.zeros((rows, 1), F32)
        for a in range(ATT_GROUP):
            snk = jnp.where(row_head == a, sink_ref[g * ATT_GROUP + a], snk)
        mx = jnp.maximum(jnp.maximum(jnp.max(s_win, axis=-1, keepdims=True),
                                     jnp.max(s_ctx, axis=-1, keepdims=True)), snk)
        p_win = jnp.exp(s_win - mx)
        p_ctx = jnp.exp(s_ctx - mx)
        den = (jnp.sum(p_win, axis=-1, keepdims=True) + jnp.sum(p_ctx, axis=-1, keepdims=True)
               + jnp.exp(snk - mx))
        acc = (jnp.dot(p_win.astype(BF16), vw, preferred_element_type=F32)
               + jnp.dot(p_ctx.astype(BF16), vc_ref[:, ks], preferred_element_type=F32))
        out = acc / den
        for a in range(ATT_GROUP):
            h = g * ATT_GROUP + a
            o_ref[:, h * HEAD_DIM:(h + 1) * HEAD_DIM] = out[a * ATT_BLOCK:(a + 1) * ATT_BLOCK]


def _attention(qk, v, sink, batch, seq, ctx_len):
    m = qk.shape[0]
    nb = seq // ATT_BLOCK
    ncb = ctx_len // ATT_BLOCK
    ctx_blk0 = batch * nb
    kcol = ATT_WIDTH // KV_WIDTH

    def qrow(b, n):
        return jnp.where(n < nb, b * nb + n, ctx_blk0 + b * ncb + (n - nb))

    def krow(off):
        def f(b, n, *_):
            return (b * nb + jnp.clip(n + off, 0, nb - 1), kcol)
        return f

    def vrow(off):
        def f(b, n, *_):
            return (b * nb + jnp.clip(n + off, 0, nb - 1), 0)
        return f

    ctx_row = (batch * seq) // ctx_len
    grid_spec = pltpu.PrefetchScalarGridSpec(
        num_scalar_prefetch=1,
        grid=(batch, nb + ncb),
        in_specs=[pl.BlockSpec((ATT_BLOCK, ATT_WIDTH), lambda b, n, *_: (qrow(b, n), 0)),
                  pl.BlockSpec((ATT_BLOCK, KV_WIDTH), krow(-1)),
                  pl.BlockSpec((ATT_BLOCK, KV_WIDTH), krow(0)),
                  pl.BlockSpec((ATT_BLOCK, KV_WIDTH), krow(1)),
                  pl.BlockSpec((ATT_BLOCK, KV_WIDTH), vrow(-1)),
                  pl.BlockSpec((ATT_BLOCK, KV_WIDTH), vrow(0)),
                  pl.BlockSpec((ATT_BLOCK, KV_WIDTH), vrow(1)),
                  pl.BlockSpec((ctx_len, KV_WIDTH), lambda b, n, *_: (ctx_row + b, kcol)),
                  pl.BlockSpec((ctx_len, KV_WIDTH), lambda b, n, *_: (ctx_row + b, 0))],
        out_specs=pl.BlockSpec((ATT_BLOCK, ATT_WIDTH), lambda b, n, *_: (qrow(b, n), 0)),
    )
    return pl.pallas_call(
        functools.partial(_attn_body, nb=nb),
        grid_spec=grid_spec,
        out_shape=jax.ShapeDtypeStruct((m, ATT_WIDTH), F32),
        compiler_params=_cparams(("parallel", "arbitrary")),
        name="attention",
    )(sink, qk, qk, qk, qk, v, v, v, qk, v)


def _seq_edges(i, tm, n_lat, seq, ctx_len):
    r0 = i * tm
    in_lat = r0 < n_lat
    pos = jnp.where(in_lat, r0 % seq, (r0 - n_lat) % ctx_len)
    length = jnp.where(in_lat, seq, ctx_len)
    return pos == 0, pos + tm == length


def _halo_specs(tm, halo, width, n_rows):
    per = tm // halo
    last = n_rows // halo - 1
    prev = pl.BlockSpec((halo, width), lambda i: (jnp.maximum(i * per - 1, 0), 0))
    nxt = pl.BlockSpec((halo, width), lambda i: (jnp.minimum((i + 1) * per, last), 0))
    return prev, nxt


def _rwkvfeat_body(p_ref, pv_ref, nx_ref, mu_ref, w0_ref, wup_ref, a0_ref, aup_ref, gup_ref, kk_ref, ka_ref,
                   down_ref, up_ref, jv_ref, misc_ref, ext_ref, *, tm, n_lat, seq, ctx_len):
    first, last = _seq_edges(pl.program_id(0), tm, n_lat, seq, ctx_len)
    h = V7X_SUBLANES
    ext_ref[h:h + tm, :] = p_ref[...]
    ext_ref[0:h, :] = jnp.where(first, 0.0, pv_ref[...])
    ext_ref[h + tm:h + tm + h, :] = jnp.where(last, 0.0, nx_ref[...])
    p = p_ref[...]
    prev = ext_ref[h - 1:h - 1 + tm, :]
    nxt = ext_ref[h + 1:h + 1 + tm, :]
    xs = p + mu_ref[...] * (0.5 * (prev + nxt) - p)
    W = RWKV_WIDTH
    r = xs[:, 0:W]
    k = xs[:, W:2 * W]
    v = xs[:, 2 * W:3 * W]
    o = 3 * W
    wd = xs[:, o:o + N_DIR * W_LORA]
    ad = xs[:, o + N_DIR * W_LORA:o + N_DIR * (W_LORA + A_LORA)]
    gd = xs[:, o + N_DIR * (W_LORA + A_LORA):]
    w_lin = w0_ref[...] + _dot3(jnp.tanh(wd), wup_ref[...])
    w_log = -jax.nn.softplus(-w_lin) - 0.5
    decay = jnp.exp(-jnp.exp(w_log))
    a = jax.nn.sigmoid(a0_ref[...] + _dot3(ad, aup_ref[...]))
    g = _dot3(jax.nn.sigmoid(gd), gup_ref[...])
    kk = k * kk_ref[...]
    kk = kk * lax.rsqrt(_seg_reduce(kk * kk, down_ref[...], up_ref[...]) + 1e-12)
    for d in range(N_DIR):
        a_d = a[:, d * W:(d + 1) * W]
        base = d * N_JVEC * W
        jv_ref[:, base + 0 * W:base + 1 * W] = decay[:, d * W:(d + 1) * W]
        jv_ref[:, base + 1 * W:base + 2 * W] = -kk
        jv_ref[:, base + 2 * W:base + 3 * W] = kk * a_d
        jv_ref[:, base + 3 * W:base + 4 * W] = k * (1.0 + (a_d - 1.0) * ka_ref[...])
        jv_ref[:, base + 4 * W:base + 5 * W] = r
    misc_ref[:, 0:W] = r
    misc_ref[:, W:2 * W] = k
    misc_ref[:, 2 * W:3 * W] = v
    misc_ref[:, 3 * W:4 * W] = g


def _rwkvfeat(p_rw, mu, w0, wup2, a0, aup2, gup, k_k, k_a, tm, n_lat, seq, ctx_len):
    m = p_rw.shape[0]
    down, up = _seg_matrices(RWKV_WIDTH, mean=False)
    prev, nxt = _halo_specs(tm, V7X_SUBLANES, RWKV_COLS, m)
    full = lambda a: pl.BlockSpec(a.shape, lambda i: (0,) * a.ndim)
    params = [mu, w0, wup2, a0, aup2, gup, k_k, k_a, down, up]
    return pl.pallas_call(
        functools.partial(_rwkvfeat_body, tm=tm, n_lat=n_lat, seq=seq, ctx_len=ctx_len),
        grid=(m // tm,),
        in_specs=[pl.BlockSpec((tm, RWKV_COLS), lambda i: (i, 0)), prev, nxt] + [full(a) for a in params],
        out_specs=[pl.BlockSpec((tm, N_DIR * N_JVEC * RWKV_WIDTH), lambda i: (i, 0)),
                   pl.BlockSpec((tm, 4 * RWKV_WIDTH), lambda i: (i, 0))],
        out_shape=[jax.ShapeDtypeStruct((m, N_DIR * N_JVEC * RWKV_WIDTH), F32),
                   jax.ShapeDtypeStruct((m, 4 * RWKV_WIDTH), F32)],
        scratch_shapes=[pltpu.VMEM((tm + 2 * V7X_SUBLANES, RWKV_COLS), F32)],
        compiler_params=_cparams(("parallel",)),
        name="rwkvfeat",
    )(p_rw, p_rw, p_rw, *params)


def _scan_body(*refs, ts, n_chain_grp):
    nq = n_chain_grp
    rev_ref = refs[0]
    j_refs = refs[1:1 + nq]
    v_refs = refs[1 + nq:1 + 2 * nq]
    y_refs = refs[1 + 2 * nq:1 + 2 * nq + N_DIR]
    s_ref, jt_ref, vs_ref, ys_ref, xt_ref = refs[1 + 2 * nq + N_DIR:]
    W = RWKV_WIDTH
    rep = V7X_LANES // (nq * RWKV_HEADS)

    @pl.when(pl.program_id(0) == 0)
    def _():
        s_ref[...] = jnp.zeros_like(s_ref)

    def flip(x):
        hi = x.astype(BF16)
        rest = x - hi.astype(F32)
        mid = rest.astype(BF16)
        lo = (rest - mid.astype(F32)).astype(BF16)
        rev = rev_ref[...]
        return ((jnp.dot(rev, hi, preferred_element_type=F32) + jnp.dot(rev, mid, preferred_element_type=F32))
                + jnp.dot(rev, lo, preferred_element_type=F32))

    def backward(q):
        return q >= nq // N_DIR

    for vec in range(N_JVEC):
        for q in range(nq):
            x = j_refs[q][:, vec * W:(vec + 1) * W]
            xt_ref[q] = (flip(x) if backward(q) else x).T

        def jtile(j, carry, vec=vec):
            pieces = []
            for q in range(nq):
                pieces += [xt_ref[q, pl.ds(j, RWKV_HEADS, stride=HEAD_DIM), :]] * rep
            jt_ref[vec, j] = jnp.concatenate(pieces, axis=0).T
            return carry

        lax.fori_loop(0, HEAD_DIM, jtile, 0, unroll=8)

    for q in range(nq):
        x = v_refs[q][...]
        xt_ref[q] = (flip(x) if backward(q) else x).T
    for ihi in range(SCAN_IHI):
        pieces = []
        for q in range(nq):
            for ilo in range(SCAN_ILO):
                pieces.append(xt_ref[q, pl.ds(ihi * SCAN_ILO + ilo, RWKV_HEADS, stride=HEAD_DIM), :])
        vs_ref[pl.ds(ihi, ts, stride=SCAN_IHI), :] = jnp.concatenate(pieces, axis=0).T

    n_acc = 8

    def tree_sum(parts):
        while len(parts) > 1:
            parts = [parts[k] + parts[k + 1] for k in range(0, len(parts), 2)]
        return parts[0]

    def add_part(parts, k, p):
        parts[k % n_acc] = p if parts[k % n_acc] is None else parts[k % n_acc] + p

    def step(t, sa):
        row = pl.ds(t, 1)
        nxt = pl.ds(jnp.minimum(t + 1, ts - 1), 1)
        base = pl.multiple_of(t * SCAN_IHI, SCAN_IHI)
        v = vs_ref[pl.ds(base, SCAN_IHI), :]
        yparts = [None] * n_acc
        sparts = [None] * n_acc
        for j in range(HEAD_DIM):
            sj = s_ref[j] * jt_ref[0, j, row, :] + sa * jt_ref[2, j, row, :] + v * jt_ref[3, j, row, :]
            s_ref[j] = sj
            add_part(yparts, j, sj * jt_ref[4, j, row, :])
            add_part(sparts, j, sj * jt_ref[1, j, nxt, :])
        ys_ref[pl.ds(base, SCAN_IHI), :] = tree_sum(yparts)
        return tree_sum(sparts)

    first = [None] * n_acc
    for j in range(HEAD_DIM):
        add_part(first, j, s_ref[j] * jt_ref[1, j, pl.ds(0, 1), :])
    lax.fori_loop(0, ts, step, tree_sum(first), unroll=2)

    for ihi in range(SCAN_IHI):
        yt = ys_ref[pl.ds(ihi, ts, stride=SCAN_IHI), :].T
        for q in range(nq):
            for ilo in range(SCAN_ILO):
                r0 = (q * SCAN_ILO + ilo) * RWKV_HEADS
                xt_ref[q, pl.ds(ihi * SCAN_ILO + ilo, RWKV_HEADS, stride=HEAD_DIM), :] = yt[r0:r0 + RWKV_HEADS]
    per_dir = nq // N_DIR
    for q in range(nq):
        y = xt_ref[q].T
        y_refs[q // per_dir][q % per_dir] = flip(y) if backward(q) else y


def _rwkv_scan(jv, misc, batch, seq, ctx_len, ts):
    nq = N_DIR * batch
    assert nq * RWKV_HEADS * SCAN_ILO == V7X_LANES and seq % ts == 0 and ctx_len % ts == 0
    n_l, n_c = seq // ts, ctx_len // ts
    W = RWKV_WIDTH

    def row_block(d, b):
        def f(i):
            is_ctx = i < n_c
            k_ctx = (n_c - 1 - i) if d else i
            k_lat = (n_l - 1 - (i - n_c)) if d else (i - n_c)
            return jnp.where(is_ctx, batch * n_l + b * n_c + k_ctx, b * n_l + k_lat)
        return f

    def pos_block(d):
        def f(i):
            return jnp.where(i < n_c, n_c - 1 - i, n_c + n_l - 1 - (i - n_c)) if d else i
        return f

    chains = [(d, b) for d in range(N_DIR) for b in range(batch)]
    j_specs = [pl.BlockSpec((ts, N_JVEC * W), lambda i, d=d, f=row_block(d, b): (f(i), d)) for d, b in chains]
    v_specs = [pl.BlockSpec((ts, W), lambda i, f=row_block(d, b): (f(i), 2)) for d, b in chains]
    y_specs = [pl.BlockSpec((batch, ts, W), lambda i, f=pos_block(d): (0, f(i), 0)) for d in range(N_DIR)]
    rev = jnp.asarray(np.eye(ts, dtype=np.float32)[::-1].copy(), BF16)
    return pl.pallas_call(
        functools.partial(_scan_body, ts=ts, n_chain_grp=nq),
        grid=(n_c + n_l,),
        in_specs=[pl.BlockSpec((ts, ts), lambda i: (0, 0))] + j_specs + v_specs,
        out_specs=y_specs,
        out_shape=[jax.ShapeDtypeStruct((batch, seq + ctx_len, W), F32)] * N_DIR,
        scratch_shapes=[pltpu.VMEM((HEAD_DIM, SCAN_IHI, V7X_LANES), F32),
                        pltpu.VMEM((N_JVEC, HEAD_DIM, ts, V7X_LANES), F32),
                        pltpu.VMEM((ts * SCAN_IHI, V7X_LANES), F32),
                        pltpu.VMEM((ts * SCAN_IHI, V7X_LANES), F32),
                        pltpu.VMEM((nq, W, ts), F32)],
        compiler_params=_cparams(("arbitrary",)),
        name="scan",
    )(rev, *([jv] * nq), *([misc] * nq))


def _rwkvout_body(yf_ref, yb_ref, misc_ref, rk_ref, lg_ref, lb_ref, dmean_ref, dsum_ref, up_ref, o_ref):
    W = RWKV_WIDTH
    y = yf_ref[0] + yb_ref[0]
    r = misc_ref[:, 0:W]
    k = misc_ref[:, W:2 * W]
    v = misc_ref[:, 2 * W:3 * W]
    g = misc_ref[:, 3 * W:4 * W]
    mean = _seg_reduce(y, dmean_ref[...], up_ref[...])
    yc = y - mean
    var = _seg_reduce(yc * yc, dmean_ref[...], up_ref[...])
    yn = yc * lax.rsqrt(var + GN_EPS) * lg_ref[...] + lb_ref[...]
    bonus = _seg_reduce(r * k * rk_ref[...], dsum_ref[...], up_ref[...]) * v
    o_ref[...] = (yn + bonus) * g


def _rwkvout(y2, misc, r_k, ln_g, ln_b, tm, n_lat, seq, ctx_len):
    m = misc.shape[0]

    def ypos(i):
        r0 = i * tm
        in_lat = r0 < n_lat
        b = jnp.where(in_lat, r0 // seq, (r0 - n_lat) // ctx_len)
        pos = jnp.where(in_lat, ctx_len + r0 % seq, (r0 - n_lat) % ctx_len)
        return (b, pos // tm, 0)

    dmean, up = _seg_matrices(RWKV_WIDTH, mean=True)
    dsum, _ = _seg_matrices(RWKV_WIDTH, mean=False)
    full = lambda a: pl.BlockSpec(a.shape, lambda i: (0,) * a.ndim)
    params = [r_k, ln_g, ln_b, dmean, dsum, up]
    return pl.pallas_call(
        _rwkvout_body,
        grid=(m // tm,),
        in_specs=[pl.BlockSpec((1, tm, RWKV_WIDTH), ypos), pl.BlockSpec((1, tm, RWKV_WIDTH), ypos),
                  pl.BlockSpec((tm, 4 * RWKV_WIDTH), lambda i: (i, 0))] + [full(a) for a in params],
        out_specs=pl.BlockSpec((tm, RWKV_WIDTH), lambda i: (i, 0)),
        out_shape=jax.ShapeDtypeStruct((m, RWKV_WIDTH), F32),
        compiler_params=_cparams(("parallel",)),
        name="rwkvout",
    )(y2[0], y2[1], misc, *params)


def _conv_body(p_ref, pv_ref, nx_ref, w_ref, b_ref, g_ref, o_ref, ext_ref, *, tm, n_lat, seq, ctx_len):
    first, last = _seq_edges(pl.program_id(0), tm, n_lat, seq, ctx_len)
    C = CONV_WIDTH
    H = CONV_HALO

    def glu(t):
        return t[:, :C] * jax.nn.sigmoid(t[:, C:])

    ext_ref[H:H + tm, :] = glu(p_ref[...])
    ext_ref[0:H, :] = jnp.where(first, 0.0, glu(pv_ref[...]))
    ext_ref[H + tm:H + tm + H, :] = jnp.where(last, 0.0, glu(nx_ref[...]))
    rows = 64
    for c in range(tm // rows):
        acc = jnp.zeros((rows, C), F32)
        for t in range(CONV_KSIZE):
            off = c * rows + H - CONV_PAD + t
            acc = acc + ext_ref[off:off + rows, :] * w_ref[t:t + 1, :]
        u = acc + b_ref[...]
        ms = jnp.mean(u * u, axis=-1, keepdims=True)
        z = u * lax.rsqrt(ms + EPS) * g_ref[...]
        o_ref[c * rows:(c + 1) * rows, :] = z * jax.nn.sigmoid(z)


def _conv(p_cv, w, b, g, tm, n_lat, seq, ctx_len):
    m = p_cv.shape[0]
    prev, nxt = _halo_specs(tm, CONV_HALO, 2 * CONV_WIDTH, m)
    full = lambda a: pl.BlockSpec(a.shape, lambda i: (0,) * a.ndim)
    return pl.pallas_call(
        functools.partial(_conv_body, tm=tm, n_lat=n_lat, seq=seq, ctx_len=ctx_len),
        grid=(m // tm,),
        in_specs=[pl.BlockSpec((tm, 2 * CONV_WIDTH), lambda i: (i, 0)), prev, nxt, full(w), full(b), full(g)],
        out_specs=pl.BlockSpec((tm, CONV_WIDTH), lambda i: (i, 0)),
        out_shape=jax.ShapeDtypeStruct((m, CONV_WIDTH), F32),
        scratch_shapes=[pltpu.VMEM((tm + 2 * CONV_HALO, CONV_WIDTH), F32)],
        compiler_params=_cparams(("parallel",)),
        name="conv",
    )(p_cv, p_cv, p_cv, w, b, g)


def _outproj_body(ya_ref, yr_ref, yc_ref, x_ref, g_ref, wa_ref, wr_ref, wc_ref, o_ref):
    acc = jnp.dot(ya_ref[...].astype(BF16), wa_ref[...], preferred_element_type=F32)
    acc += jnp.dot(yr_ref[...].astype(BF16), wr_ref[...], preferred_element_type=F32)
    acc += jnp.dot(yc_ref[...].astype(BF16), wc_ref[...], preferred_element_type=F32)
    o_ref[...] = x_ref[...] + g_ref[0] * acc


def _outproj(ya, yr, yc, x, gate, w, n_lat, seq, tm, tn):
    m, d = x.shape
    n_grp = gate.shape[0] - 1

    def grp(i, j):
        r0 = i * tm
        return (jnp.where(r0 < n_lat, r0 // seq, n_grp), 0, j)

    wa, wr, wc = w[:ATT_WIDTH], w[ATT_WIDTH:ATT_WIDTH + RWKV_WIDTH], w[ATT_WIDTH + RWKV_WIDTH:]
    return pl.pallas_call(
        _outproj_body,
        grid=(m // tm, d // tn),
        in_specs=[pl.BlockSpec((tm, ATT_WIDTH), lambda i, j: (i, 0)),
                  pl.BlockSpec((tm, RWKV_WIDTH), lambda i, j: (i, 0)),
                  pl.BlockSpec((tm, CONV_WIDTH), lambda i, j: (i, 0)),
                  pl.BlockSpec((tm, tn), lambda i, j: (i, j)),
                  pl.BlockSpec((1, 1, tn), grp),
                  pl.BlockSpec((ATT_WIDTH, tn), lambda i, j: (0, j)),
                  pl.BlockSpec((RWKV_WIDTH, tn), lambda i, j: (0, j)),
                  pl.BlockSpec((CONV_WIDTH, tn), lambda i, j: (0, j))],
        out_specs=pl.BlockSpec((tm, tn), lambda i, j: (i, j)),
        out_shape=jax.ShapeDtypeStruct((m, d), F32),
        compiler_params=_cparams(("parallel", "arbitrary")),
        name="outproj",
    )(ya, yr, yc, x, gate, wa, wr, wc)


def _peer_pairs():
    n = PEER_TOPK + 1
    return [(a, b) for a in range(n) for b in range(n) if (a + 1) * (b + 1) <= n]


def _top_distinct(s, n, count):
    vals, cnts = [], []
    v = s
    for _ in range(n):
        mx = jnp.max(v, axis=0, keepdims=True)
        hit = v == mx
        vals.append(mx)
        if count:
            cnt = jnp.sum(hit.astype(F32), axis=0, keepdims=True)
            cnts.append(jnp.where(mx == -jnp.inf, 0.0, cnt))
        v = jnp.where(hit, -jnp.inf, v)
    if count:
        return vals, cnts
    return vals, jnp.sum((v == -jnp.inf).astype(F32), axis=0, keepdims=True)


def _peer_head_factors(q_ref, keys_ref, h, count):
    n = PEER_TOPK + 1
    nt = (((1,), (1,)), ((), ()))
    pairs = _peer_pairs()
    sc = []
    for p in range(2):
        c0 = (h * 2 + p) * PEER_N_KEYS
        sc.append(lax.dot_general(keys_ref[h, p], q_ref[:, c0:c0 + PEER_N_KEYS], nt,
                                  preferred_element_type=F32, precision=HIGHEST))
    (v1, n1), (v2, n2) = _top_distinct(sc[0], n, count), _top_distinct(sc[1], n, count)
    cv = jnp.concatenate([v1[a] + v2[b] for a, b in pairs], axis=0)
    cm = jnp.concatenate([n1[a] * n2[b] for a, b in pairs], axis=0) if count else None
    top = v1[0] + v2[0]
    cum = jnp.zeros_like(top)
    t16 = jnp.full_like(top, -jnp.inf)
    t17 = jnp.full_like(top, -jnp.inf)
    z = jnp.zeros_like(top)
    for _ in range(n):
        mx = jnp.max(cv, axis=0, keepdims=True)
        hit = cv == mx
        hits = jnp.where(hit, cm, 0.0) if count else hit.astype(F32)
        cnt = jnp.where(mx == -jnp.inf, 0.0, jnp.sum(hits, axis=0, keepdims=True))
        new = cum + cnt
        t16 = jnp.where((cum < PEER_TOPK) & (new >= PEER_TOPK), mx, t16)
        t17 = jnp.where((cum < n) & (new >= n), mx, t17)
        take = jnp.minimum(cnt, jnp.maximum(PEER_TOPK - cum, 0.0))
        z = z + jnp.where(take > 0, take * jnp.exp(mx - top), 0.0)
        cum = new
        cv = jnp.where(hit, -jnp.inf, cv)
    thr = 0.5 * (t16 + t17)
    factors = (jnp.exp(sc[1] - v2[0]), jnp.exp(thr - sc[0] - v2[0]), jnp.exp(sc[0] - v1[0]) / z)
    if count:
        return factors, None
    return factors, (n1 != n).astype(F32) + (n2 != n).astype(F32)


def _peertopk_body(q_ref, keys_ref, e2_ref, a1_ref, e1_ref):
    dup = None
    for h in range(PEER_HEADS):
        (e2_ref[h], a1_ref[h], e1_ref[h]), d = _peer_head_factors(q_ref, keys_ref, h, count=False)
        dup = d if dup is None else dup + d

    @pl.when(jnp.max(dup) > 0)
    def _():
        for h in range(PEER_HEADS):
            (e2_ref[h], a1_ref[h], e1_ref[h]), _ = _peer_head_factors(q_ref, keys_ref, h, count=True)


def _peertopk(q, keys, tt):
    m, qw = q.shape
    shp = jax.ShapeDtypeStruct((PEER_HEADS, PEER_N_KEYS, m), F32)
    ospec = pl.BlockSpec((PEER_HEADS, PEER_N_KEYS, tt), lambda i: (0, 0, i))
    return pl.pallas_call(
        _peertopk_body,
        grid=(m // tt,),
        in_specs=[pl.BlockSpec((tt, qw), lambda i: (i, 0)),
                  pl.BlockSpec(keys.shape, lambda i: (0, 0, 0, 0))],
        out_specs=[ospec] * 3,
        out_shape=[shp] * 3,
        compiler_params=_cparams(("parallel",)),
        name="peertopk",
    )(q, keys)


def _peerdense_body(h_ref, u_ref, vt_ref, e2_ref, a1_ref, e1_ref, x_ref, g_ref, o_ref, acc_ref, a_ref, ht_ref,
                    st_ref, *, ec):
    e = pl.program_id(1)
    tt = ht_ref.shape[1]

    @pl.when(e == 0)
    def _():
        acc_ref[...] = jnp.zeros_like(acc_ref)
        ht_ref[...] = h_ref[...].astype(F32).T.astype(BF16)

    st_ref[...] = jnp.dot(u_ref[...], ht_ref[...], preferred_element_type=F32)
    per = ec // PEER_N_KEYS
    tc = 2 * V7X_LANES
    rh = PEER_N_KEYS // 2
    for c in range(tt // tc):
        cs = pl.ds(c * tc, tc)
        for ii in range(per):
            i1 = e * per + ii
            for r in range(PEER_N_KEYS // rh):
                ks = pl.ds(r * rh, rh)
                w = None
                for h in range(PEER_HEADS):
                    e2 = e2_ref[h, ks, cs]
                    t = jnp.where(e2 >= a1_ref[h, pl.ds(i1, 1), cs], e2, 0.0) * e1_ref[h, pl.ds(i1, 1), cs]
                    w = t if w is None else w + t
                rows = pl.ds(ii * PEER_N_KEYS + r * rh, rh)
                s = st_ref[rows, cs]
                act = 0.5 * s * (1.0 + lax.erf(s * np.float32(1.0 / np.sqrt(2.0))))
                a_ref[rows, cs] = (act * w).astype(BF16)
    acc_ref[...] += jnp.dot(vt_ref[...], a_ref[...], preferred_element_type=F32)

    @pl.when(e == pl.num_programs(1) - 1)
    def _():
        o_ref[...] = x_ref[...] + g_ref[0] * acc_ref[...].T


def _peerdense(h2, u, vt, e2, a1, e1, x, gate, n_lat, seq, tt, ec, m_out):
    d = x.shape[1]
    m = m_out
    n_chunks = u.shape[0] // ec
    n_grp = gate.shape[0] - 1

    def grp(i, e):
        r0 = i * tt
        return (jnp.where(r0 < n_lat, r0 // seq, n_grp), 0, 0)

    once = pl.Buffered(1)
    tok = pl.BlockSpec((PEER_HEADS, PEER_N_KEYS, tt), lambda i, e: (0, 0, i), pipeline_mode=once)
    return pl.pallas_call(
        functools.partial(_peerdense_body, ec=ec),
        grid=(m // tt, n_chunks),
        in_specs=[pl.BlockSpec((tt, d), lambda i, e: (i, 0), pipeline_mode=once),
                  pl.BlockSpec((ec, d), lambda i, e: (e, 0)),
                  pl.BlockSpec((d, ec), lambda i, e: (0, e)),
                  tok, tok, tok,
                  pl.BlockSpec((tt, d), lambda i, e: (i, 0), pipeline_mode=once),
                  pl.BlockSpec((1, 1, d), grp)],
        out_specs=pl.BlockSpec((tt, d), lambda i, e: (i, 0)),
        out_shape=jax.ShapeDtypeStruct((m, d), F32),
        scratch_shapes=[pltpu.VMEM((d, tt), F32), pltpu.VMEM((ec, tt), BF16), pltpu.VMEM((d, tt), BF16),
                        pltpu.VMEM((ec, tt), F32)],
        compiler_params=_cparams(("parallel", "arbitrary")),
        name="peerdense",
    )(h2, u, vt, e2, a1, e1, x, gate)


def _tiles(seq, ctx_len, batch):
    ctx_rows = batch * ctx_len
    return dict(
        mm=min(512, seq, ctx_rows),
        seqt=min(256, seq, ctx_len),
        peer=min(512, seq, ctx_rows),
        scan=min(128, seq, ctx_len),
    )


def _rope_tables(batch, seq, ctx_len):
    inv = ROPE_THETA ** (-jnp.arange(0, ROPE_AXIS_DIM, 2, dtype=F32) / ROPE_AXIS_DIM)
    t = jnp.arange(seq, dtype=jnp.int32)
    ang_r = (t // GRID_W).astype(F32)[:, None] * inv[None, :]
    ang_c = (t % GRID_W).astype(F32)[:, None] * inv[None, :]
    cos = jnp.concatenate([jnp.cos(ang_r)] * 2 + [jnp.cos(ang_c)] * 2, axis=-1)
    sin = jnp.concatenate([-jnp.sin(ang_r), jnp.sin(ang_r), -jnp.sin(ang_c), jnp.sin(ang_c)], axis=-1)
    n_ctx = batch * ctx_len
    cos = jnp.concatenate([jnp.tile(cos, (batch, 1)), jnp.ones((n_ctx, HEAD_DIM), F32)], axis=0)
    sin = jnp.concatenate([jnp.tile(sin, (batch, 1)), jnp.zeros((n_ctx, HEAD_DIM), F32)], axis=0)
    return jnp.tile(cos, (1, 2)), jnp.tile(sin, (1, 2))


def _block_diag2(w):
    z = jnp.zeros_like(w[0])
    return jnp.concatenate([jnp.concatenate([w[0], z], axis=1), jnp.concatenate([z, w[1]], axis=1)], axis=0)


def kernel(x, c, ctx, c_ctx, norm1_g, norm2_g, w_ada, b_ada, w_in, w_out, q_norm_g, k_norm_g, attn_sink, rwkv_mu, rwkv_w0, rwkv_w_up, rwkv_a0, rwkv_a_up, rwkv_g_up, rwkv_k_k, rwkv_k_a, rwkv_r_k, rwkv_ln_g, rwkv_ln_b, conv_w, conv_b, conv_norm_g, peer_wq, peer_keys, peer_u, peer_v):
    batch, seq, d = x.shape
    ctx_len = ctx.shape[1]
    depth = w_in.shape[0]
    n_lat = batch * seq
    tl = _tiles(seq, ctx_len, batch)
    assert seq % ATT_BLOCK == 0 and ctx_len % ATT_BLOCK == 0 and n_lat % ctx_len == 0
    assert (seq + ctx_len) % tl["scan"] == 0

    xt = jnp.concatenate([x.reshape(n_lat, d), ctx.reshape(batch * ctx_len, d)], axis=0)
    cc = jnp.concatenate([c, c_ctx[None, :], jnp.zeros((8 - batch - 1, d), F32)], axis=0)
    cos2, sin2 = _rope_tables(batch, seq, ctx_len)
    row = lambda a: a.reshape(1, -1)

    for i in range(depth):
        mod = _ada(cc, w_ada[i], row(b_ada[i]))[:batch + 1].reshape(batch + 1, 6, 1, d)
        sh1, sc1, g1, sh2, sc2, g2 = (mod[:, k] for k in range(6))

        w_in_b = w_in[i].astype(BF16)
        p_att, h1 = _modmm(xt, row(norm1_g[i]), sc1, sh1, w_in_b[:, :ATT_COLS], n_lat, seq, tl["mm"], ATT_COLS)
        p_rw = _mm(h1, w_in_b[:, ATT_COLS:ATT_COLS + RWKV_COLS], tl["mm"], RWKV_COLS)
        p_cv = _mm(h1, w_in_b[:, ATT_COLS + RWKV_COLS:], tl["mm"], 2 * CONV_WIDTH)

        gain = jnp.concatenate([jnp.tile(q_norm_g[i] * HEAD_DIM ** -0.5, ATT_HEADS), jnp.tile(k_norm_g[i], ATT_KV_HEADS)])
        qk, vv = _qkprep(p_att, row(gain), cos2, sin2, tl["seqt"])
        y_att = _attention(qk, vv, attn_sink[i], batch, seq, ctx_len)

        jv, misc = _rwkvfeat(p_rw, row(rwkv_mu[i]), row(rwkv_w0[i]), _block_diag2(rwkv_w_up[i]),
                             row(rwkv_a0[i]), _block_diag2(rwkv_a_up[i]), rwkv_g_up[i],
                             row(rwkv_k_k[i]), row(rwkv_k_a[i]), tl["seqt"], n_lat, seq, ctx_len)
        y2 = _rwkv_scan(jv, misc, batch, seq, ctx_len, tl["scan"])
        y_rw = _rwkvout(y2, misc, row(rwkv_r_k[i]), row(rwkv_ln_g[i]), row(rwkv_ln_b[i]), tl["seqt"],
                        n_lat, seq, ctx_len)

        y_cv = _conv(p_cv, conv_w[i], row(conv_b[i]), row(conv_norm_g[i]), tl["seqt"], n_lat, seq, ctx_len)

        xt = _outproj(y_att, y_rw, y_cv, xt, g1, w_out[i].astype(BF16), n_lat, seq, tl["mm"], d)

        q, h2 = _modmm(xt, row(norm2_g[i]), sc2, sh2, peer_wq[i].astype(BF16), n_lat, seq, tl["mm"], peer_wq.shape[2])
        e2, a1, e1 = _peertopk(q, peer_keys[i], tl["peer"])
        xt = _peerdense(h2, peer_u[i].astype(BF16), peer_v[i].T.astype(BF16), e2, a1, e1, xt, g2,
                        n_lat, seq, tl["peer"], 1024, n_lat if i == depth - 1 else xt.shape[0])
    return xt.reshape(batch, seq, d)
```

```python
import functools

import jax
import jax.numpy as jnp
import numpy as np
from jax import lax
from jax.experimental import pallas as pl
from jax.experimental.pallas import tpu as pltpu

F32 = jnp.float32
BF16 = jnp.bfloat16
HIGHEST = lax.Precision.HIGHEST

V7X_LANES = 128
V7X_SUBLANES = 8
V7X_VMEM_BYTES = 64 * 1024 * 1024
VMEM_LIMIT = V7X_VMEM_BYTES - 8 * 1024 * 1024

HEAD_DIM = 64
EPS = 1e-6
GN_EPS = 64e-5
NEG_INF = -1e30
GRID_W = 64
ATT_HEADS = 16
ATT_KV_HEADS = 4
ATT_GROUP = ATT_HEADS // ATT_KV_HEADS
ATT_WIDTH = ATT_HEADS * HEAD_DIM
KV_WIDTH = ATT_KV_HEADS * HEAD_DIM
QK_WIDTH = ATT_WIDTH + KV_WIDTH
ATT_COLS = ATT_WIDTH + 2 * KV_WIDTH
ATT_BLOCK = 128
ROPE_THETA = 10000.0
ROPE_AXIS_DIM = HEAD_DIM // 2
ROPE_HALF = ROPE_AXIS_DIM // 2
RWKV_HEADS = 8
RWKV_WIDTH = RWKV_HEADS * HEAD_DIM
W_LORA = 64
A_LORA = 64
G_LORA = 128
N_DIR = 2
RWKV_COLS = 3 * RWKV_WIDTH + N_DIR * W_LORA + N_DIR * A_LORA + G_LORA
CONV_WIDTH = 512
CONV_KSIZE = 31
CONV_PAD = (CONV_KSIZE - 1) // 2
CONV_HALO = 16
PEER_HEADS = 8
PEER_N_KEYS = 128
PEER_TOPK = 16
N_JVEC = 5
SCAN_ILO = 4
SCAN_IHI = HEAD_DIM // SCAN_ILO


def _cparams(sem):
    return pltpu.CompilerParams(dimension_semantics=sem, vmem_limit_bytes=VMEM_LIMIT)


def _seg_matrices(width, mean):
    nseg = width // HEAD_DIM
    seg = np.arange(width) // HEAD_DIM
    down = np.zeros((width, V7X_LANES), np.float32)
    down[np.arange(width), seg] = 1.0 / HEAD_DIM if mean else 1.0
    up = np.zeros((V7X_LANES, width), np.float32)
    up[seg, np.arange(width)] = 1.0
    assert nseg <= V7X_LANES
    return jnp.asarray(down, BF16), jnp.asarray(up, BF16)


def _split2(a):
    hi = a.astype(BF16)
    return hi, (a - hi.astype(F32)).astype(BF16)


def _dot_exact_rhs(a, m):
    hi, lo = _split2(a)
    return jnp.dot(hi, m, preferred_element_type=F32) + jnp.dot(lo, m, preferred_element_type=F32)


def _dot3(a, b):
    a_hi, a_lo = _split2(a)
    b_hi, b_lo = _split2(b)
    return (jnp.dot(a_hi, b_hi, preferred_element_type=F32) + jnp.dot(a_lo, b_hi, preferred_element_type=F32)
            + jnp.dot(a_hi, b_lo, preferred_element_type=F32))


def _seg_reduce(x, down, up):
    return _dot_exact_rhs(_dot_exact_rhs(x, down), up)


def _ada_body(c_ref, w_ref, b_ref, o_ref):
    cc = c_ref[...]
    s = (cc * jax.nn.sigmoid(cc)).astype(BF16)
    o_ref[...] = jnp.dot(s, w_ref[...].astype(BF16), preferred_element_type=F32) + b_ref[...]


def _ada(cc, w, b):
    d, n = w.shape
    tn = 1024
    return pl.pallas_call(
        _ada_body,
        grid=(n // tn,),
        in_specs=[pl.BlockSpec((8, d), lambda j: (0, 0)),
                  pl.BlockSpec((d, tn), lambda j: (0, j)),
                  pl.BlockSpec((1, tn), lambda j: (0, j))],
        out_specs=pl.BlockSpec((8, tn), lambda j: (0, j)),
        out_shape=jax.ShapeDtypeStruct((8, n), F32),
        compiler_params=_cparams(("arbitrary",)),
        name="ada",
    )(cc, w, b)


def _modmm_body(x_ref, g_ref, sc_ref, sh_ref, w_ref, o_ref, h_ref):
    @pl.when(pl.program_id(1) == 0)
    def _():
        x = x_ref[...]
        ms = jnp.mean(x * x, axis=-1, keepdims=True)
        y = x * lax.rsqrt(ms + EPS) * g_ref[...]
        h_ref[...] = (y * (1.0 + sc_ref[0]) + sh_ref[0]).astype(BF16)

    o_ref[...] = jnp.dot(h_ref[...], w_ref[...], preferred_element_type=F32)


def _modmm(x, g, sc, sh, w, n_lat, seq, tm, tn):
    m, d = x.shape
    n = w.shape[1]
    n_grp = sc.shape[0] - 1

    def grp(i, j):
        r0 = i * tm
        return (jnp.where(r0 < n_lat, r0 // seq, n_grp), 0, 0)

    return pl.pallas_call(
        _modmm_body,
        grid=(m // tm, n // tn),
        in_specs=[pl.BlockSpec((tm, d), lambda i, j: (i, 0)),
                  pl.BlockSpec((1, d), lambda i, j: (0, 0)),
                  pl.BlockSpec((1, 1, d), grp),
                  pl.BlockSpec((1, 1, d), grp),
                  pl.BlockSpec((d, tn), lambda i, j: (0, j))],
        out_specs=[pl.BlockSpec((tm, tn), lambda i, j: (i, j)),
                   pl.BlockSpec((tm, d), lambda i, j: (i, 0))],
        out_shape=[jax.ShapeDtypeStruct((m, n), F32), jax.ShapeDtypeStruct((m, d), BF16)],
        compiler_params=_cparams(("parallel", "arbitrary")),
        name="modmm",
    )(x, g, sc, sh, w)


def _mm_body(h_ref, w_ref, o_ref):
    o_ref[...] = jnp.dot(h_ref[...], w_ref[...], preferred_element_type=F32)


def _mm(h, w, tm, tn):
    m, d = h.shape
    n = w.shape[1]
    return pl.pallas_call(
        _mm_body,
        grid=(m // tm, n // tn),
        in_specs=[pl.BlockSpec((tm, d), lambda i, j: (i, 0)),
                  pl.BlockSpec((d, tn), lambda i, j: (0, j))],
        out_specs=pl.BlockSpec((tm, tn), lambda i, j: (i, j)),
        out_shape=jax.ShapeDtypeStruct((m, n), F32),
        compiler_params=_cparams(("parallel", "arbitrary")),
        name="mm",
    )(h, w)


def _qkprep_body(p_ref, gain_ref, cos_ref, sin_ref, down_ref, up_ref, qk_ref, v_ref):
    x = p_ref[:, :QK_WIDTH]
    ms = _seg_reduce(x * x, down_ref[...], up_ref[...])
    y = x * lax.rsqrt(ms + EPS) * gain_ref[...]
    cos = cos_ref[...]
    sin = sin_ref[...]
    lane = lax.broadcasted_iota(jnp.int32, cos.shape, 1)
    first = (lane % ROPE_AXIS_DIM) < ROPE_HALF
    for blk in range(QK_WIDTH // V7X_LANES):
        yb = y[:, blk * V7X_LANES:(blk + 1) * V7X_LANES]
        partner = jnp.where(first, pltpu.roll(yb, V7X_LANES - ROPE_HALF, 1), pltpu.roll(yb, ROPE_HALF, 1))
        qk_ref[:, blk * V7X_LANES:(blk + 1) * V7X_LANES] = (yb * cos + partner * sin).astype(BF16)
    v_ref[...] = p_ref[:, QK_WIDTH:].astype(BF16)


def _qkprep(p_att, gain, cos2, sin2, tm):
    m = p_att.shape[0]
    down, up = _seg_matrices(QK_WIDTH, mean=True)
    return pl.pallas_call(
        _qkprep_body,
        grid=(m // tm,),
        in_specs=[pl.BlockSpec((tm, ATT_COLS), lambda i: (i, 0)),
                  pl.BlockSpec((1, QK_WIDTH), lambda i: (0, 0)),
                  pl.BlockSpec((tm, V7X_LANES), lambda i: (i, 0)),
                  pl.BlockSpec((tm, V7X_LANES), lambda i: (i, 0)),
                  pl.BlockSpec(down.shape, lambda i: (0, 0)),
                  pl.BlockSpec(up.shape, lambda i: (0, 0))],
        out_specs=[pl.BlockSpec((tm, QK_WIDTH), lambda i: (i, 0)),
                   pl.BlockSpec((tm, KV_WIDTH), lambda i: (i, 0))],
        out_shape=[jax.ShapeDtypeStruct((m, QK_WIDTH), BF16), jax.ShapeDtypeStruct((m, KV_WIDTH), BF16)],
        compiler_params=_cparams(("parallel",)),
        name="qkprep",
    )(p_att, gain, cos2, sin2, down, up)


def _attn_body(sink_ref, q_ref, k0_ref, k1_ref, k2_ref, v0_ref, v1_ref, v2_ref, kc_ref, vc_ref, o_ref, *, nb):
    n = pl.program_id(1)
    is_lat = n < nb
    rows = ATT_GROUP * ATT_BLOCK
    qi = lax.broadcasted_iota(jnp.int32, (rows, 3 * ATT_BLOCK), 0) % ATT_BLOCK
    si = lax.broadcasted_iota(jnp.int32, (rows, 3 * ATT_BLOCK), 1)
    kblk = n - 1 + si // ATT_BLOCK
    mask = (jnp.abs(si - ATT_BLOCK - qi) <= ATT_BLOCK) & (kblk >= 0) & (kblk < nb) & is_lat
    nt = (((1,), (1,)), ((), ()))
    for g in range(ATT_KV_HEADS):
        ks = slice(g * HEAD_DIM, (g + 1) * HEAD_DIM)
        qs = jnp.concatenate(
            [q_ref[:, (g * ATT_GROUP + a) * HEAD_DIM:(g * ATT_GROUP + a + 1) * HEAD_DIM] for a in range(ATT_GROUP)],
            axis=0)
        kw = jnp.concatenate([k0_ref[:, ks], k1_ref[:, ks], k2_ref[:, ks]], axis=0)
        vw = jnp.concatenate([v0_ref[:, ks], v1_ref[:, ks], v2_ref[:, ks]], axis=0)
        s_win = jnp.where(mask, lax.dot_general(qs, kw, nt, preferred_element_type=F32), NEG_INF)
        s_ctx = lax.dot_general(qs, kc_ref[:, ks], nt, preferred_element_type=F32)
        row_head = lax.broadcasted_iota(jnp.int32, (rows, 1), 0) // ATT_BLOCK
        snk = jnp.zeros((rows, 1), F32)
        for a in range(ATT_GROUP):
            snk = jnp.where(row_head == a, sink_ref[g * ATT_GROUP + a], snk)
        mx = jnp.maximum(jnp.maximum(jnp.max(s_win, axis=-1, keepdims=True),
                                     jnp.max(s_ctx, axis=-1, keepdims=True)), snk)
        p_win = jnp.exp(s_win - mx)
        p_ctx = jnp.exp(s_ctx - mx)
        den = (jnp.sum(p_win, axis=-1, keepdims=True) + jnp.sum(p_ctx, axis=-1, keepdims=True)
               + jnp.exp(snk - mx))
        acc = (jnp.dot(p_win.astype(BF16), vw, preferred_element_type=F32)
               + jnp.dot(p_ctx.astype(BF16), vc_ref[:, ks], preferred_element_type=F32))
        out = acc / den
        for a in range(ATT_GROUP):
            h = g * ATT_GROUP + a
            o_ref[:, h * HEAD_DIM:(h + 1) * HEAD_DIM] = out[a * ATT_BLOCK:(a + 1) * ATT_BLOCK]


def _attention(qk, v, sink, batch, seq, ctx_len):
    m = qk.shape[0]
    nb = seq // ATT_BLOCK
    ncb = ctx_len // ATT_BLOCK
    ctx_blk0 = batch * nb
    kcol = ATT_WIDTH // KV_WIDTH

    def qrow(b, n):
        return jnp.where(n < nb, b * nb + n, ctx_blk0 + b * ncb + (n - nb))

    def krow(off):
        def f(b, n, *_):
            return (b * nb + jnp.clip(n + off, 0, nb - 1), kcol)
        return f

    def vrow(off):
        def f(b, n, *_):
            return (b * nb + jnp.clip(n + off, 0, nb - 1), 0)
        return f

    ctx_row = (batch * seq) // ctx_len
    grid_spec = pltpu.PrefetchScalarGridSpec(
        num_scalar_prefetch=1,
        grid=(batch, nb + ncb),
        in_specs=[pl.BlockSpec((ATT_BLOCK, ATT_WIDTH), lambda b, n, *_: (qrow(b, n), 0)),
                  pl.BlockSpec((ATT_BLOCK, KV_WIDTH), krow(-1)),
                  pl.BlockSpec((ATT_BLOCK, KV_WIDTH), krow(0)),
                  pl.BlockSpec((ATT_BLOCK, KV_WIDTH), krow(1)),
                  pl.BlockSpec((ATT_BLOCK, KV_WIDTH), vrow(-1)),
                  pl.BlockSpec((ATT_BLOCK, KV_WIDTH), vrow(0)),
                  pl.BlockSpec((ATT_BLOCK, KV_WIDTH), vrow(1)),
                  pl.BlockSpec((ctx_len, KV_WIDTH), lambda b, n, *_: (ctx_row + b, kcol)),
                  pl.BlockSpec((ctx_len, KV_WIDTH), lambda b, n, *_: (ctx_row + b, 0))],
        out_specs=pl.BlockSpec((ATT_BLOCK, ATT_WIDTH), lambda b, n, *_: (qrow(b, n), 0)),
    )
    return pl.pallas_call(
        functools.partial(_attn_body, nb=nb),
        grid_spec=grid_spec,
        out_shape=jax.ShapeDtypeStruct((m, ATT_WIDTH), F32),
        compiler_params=_cparams(("parallel", "arbitrary")),
        name="attention",
    )(sink, qk, qk, qk, qk, v, v, v, qk, v)


def _seq_edges(i, tm, n_lat, seq, ctx_len):
    r0 = i * tm
    in_lat = r0 < n_lat
    pos = jnp.where(in_lat, r0 % seq, (r0 - n_lat) % ctx_len)
    length = jnp.where(in_lat, seq, ctx_len)
    return pos == 0, pos + tm == length


def _halo_specs(tm, halo, width, n_rows):
    per = tm // halo
    last = n_rows // halo - 1
    prev = pl.BlockSpec((halo, width), lambda i: (jnp.maximum(i * per - 1, 0), 0))
    nxt = pl.BlockSpec((halo, width), lambda i: (jnp.minimum((i + 1) * per, last), 0))
    return prev, nxt


def _rwkvfeat_body(p_ref, pv_ref, nx_ref, mu_ref, w0_ref, wup_ref, a0_ref, aup_ref, gup_ref, kk_ref, ka_ref,
                   down_ref, up_ref, jv_ref, misc_ref, ext_ref, *, tm, n_lat, seq, ctx_len):
    first, last = _seq_edges(pl.program_id(0), tm, n_lat, seq, ctx_len)
    h = V7X_SUBLANES
    ext_ref[h:h + tm, :] = p_ref[...]
    ext_ref[0:h, :] = jnp.where(first, 0.0, pv_ref[...])
    ext_ref[h + tm:h + tm + h, :] = jnp.where(last, 0.0, nx_ref[...])
    p = p_ref[...]
    prev = ext_ref[h - 1:h - 1 + tm, :]
    nxt = ext_ref[h + 1:h + 1 + tm, :]
    xs = p + mu_ref[...] * (0.5 * (prev + nxt) - p)
    W = RWKV_WIDTH
    r = xs[:, 0:W]
    k = xs[:, W:2 * W]
    v = xs[:, 2 * W:3 * W]
    o = 3 * W
    wd = xs[:, o:o + N_DIR * W_LORA]
    ad = xs[:, o + N_DIR * W_LORA:o + N_DIR * (W_LORA + A_LORA)]
    gd = xs[:, o + N_DIR * (W_LORA + A_LORA):]
    w_lin = w0_ref[...] + _dot3(jnp.tanh(wd), wup_ref[...])
    w_log = -jax.nn.softplus(-w_lin) - 0.5
    decay = jnp.exp(-jnp.exp(w_log))
    a = jax.nn.sigmoid(a0_ref[...] + _dot3(ad, aup_ref[...]))
    g = _dot3(jax.nn.sigmoid(gd), gup_ref[...])
    kk = k * kk_ref[...]
    kk = kk * lax.rsqrt(_seg_reduce(kk * kk, down_ref[...], up_ref[...]) + 1e-12)
    for d in range(N_DIR):
        a_d = a[:, d * W:(d + 1) * W]
        base = d * N_JVEC * W
        jv_ref[:, base + 0 * W:base + 1 * W] = decay[:, d * W:(d + 1) * W]
        jv_ref[:, base + 1 * W:base + 2 * W] = -kk
        jv_ref[:, base + 2 * W:base + 3 * W] = kk * a_d
        jv_ref[:, base + 3 * W:base + 4 * W] = k * (1.0 + (a_d - 1.0) * ka_ref[...])
        jv_ref[:, base + 4 * W:base + 5 * W] = r
    misc_ref[:, 0:W] = r
    misc_ref[:, W:2 * W] = k
    misc_ref[:, 2 * W:3 * W] = v
    misc_ref[:, 3 * W:4 * W] = g


def _rwkvfeat(p_rw, mu, w0, wup2, a0, aup2, gup, k_k, k_a, tm, n_lat, seq, ctx_len):
    m = p_rw.shape[0]
    down, up = _seg_matrices(RWKV_WIDTH, mean=False)
    prev, nxt = _halo_specs(tm, V7X_SUBLANES, RWKV_COLS, m)
    full = lambda a: pl.BlockSpec(a.shape, lambda i: (0,) * a.ndim)
    params = [mu, w0, wup2, a0, aup2, gup, k_k, k_a, down, up]
    return pl.pallas_call(
        functools.partial(_rwkvfeat_body, tm=tm, n_lat=n_lat, seq=seq, ctx_len=ctx_len),
        grid=(m // tm,),
        in_specs=[pl.BlockSpec((tm, RWKV_COLS), lambda i: (i, 0)), prev, nxt] + [full(a) for a in params],
        out_specs=[pl.BlockSpec((tm, N_DIR * N_JVEC * RWKV_WIDTH), lambda i: (i, 0)),
                   pl.BlockSpec((tm, 4 * RWKV_WIDTH), lambda i: (i, 0))],
        out_shape=[jax.ShapeDtypeStruct((m, N_DIR * N_JVEC * RWKV_WIDTH), F32),
                   jax.ShapeDtypeStruct((m, 4 * RWKV_WIDTH), F32)],
        scratch_shapes=[pltpu.VMEM((tm + 2 * V7X_SUBLANES, RWKV_COLS), F32)],
        compiler_params=_cparams(("parallel",)),
        name="rwkvfeat",
    )(p_rw, p_rw, p_rw, *params)


def _scan_body(*refs, ts, n_chain_grp):
    nq = n_chain_grp
    rev_ref = refs[0]
    j_refs = refs[1:1 + nq]
    v_refs = refs[1 + nq:1 + 2 * nq]
    y_refs = refs[1 + 2 * nq:1 + 2 * nq + N_DIR]
    s_ref, jt_ref, vs_ref, ys_ref, xt_ref = refs[1 + 2 * nq + N_DIR:]
    W = RWKV_WIDTH
    rep = V7X_LANES // (nq * RWKV_HEADS)

    @pl.when(pl.program_id(0) == 0)
    def _():
        s_ref[...] = jnp.zeros_like(s_ref)

    def flip(x):
        hi = x.astype(BF16)
        rest = x - hi.astype(F32)
        mid = rest.astype(BF16)
        lo = (rest - mid.astype(F32)).astype(BF16)
        rev = rev_ref[...]
        return ((jnp.dot(rev, hi, preferred_element_type=F32) + jnp.dot(rev, mid, preferred_element_type=F32))
                + jnp.dot(rev, lo, preferred_element_type=F32))

    def backward(q):
        return q >= nq // N_DIR

    for vec in range(N_JVEC):
        for q in range(nq):
            x = j_refs[q][:, vec * W:(vec + 1) * W]
            xt_ref[q] = (flip(x) if backward(q) else x).T

        def jtile(j, carry, vec=vec):
            pieces = []
            for q in range(nq):
                pieces += [xt_ref[q, pl.ds(j, RWKV_HEADS, stride=HEAD_DIM), :]] * rep
            jt_ref[vec, j] = jnp.concatenate(pieces, axis=0).T
            return carry

        lax.fori_loop(0, HEAD_DIM, jtile, 0, unroll=16)

    for q in range(nq):
        x = v_refs[q][...]
        xt_ref[q] = (flip(x) if backward(q) else x).T
    for ihi in range(SCAN_IHI):
        pieces = []
        for q in range(nq):
            for ilo in range(SCAN_ILO):
                pieces.append(xt_ref[q, pl.ds(ihi * SCAN_ILO + ilo, RWKV_HEADS, stride=HEAD_DIM), :])
        vs_ref[pl.ds(ihi, ts, stride=SCAN_IHI), :] = jnp.concatenate(pieces, axis=0).T

    n_acc = 8

    def tree_sum(parts):
        while len(parts) > 1:
            parts = [parts[k] + parts[k + 1] for k in range(0, len(parts), 2)]
        return parts[0]

    def add_part(parts, k, p):
        parts[k % n_acc] = p if parts[k % n_acc] is None else parts[k % n_acc] + p

    def step(t, sa):
        row = pl.ds(t, 1)
        nxt = pl.ds(jnp.minimum(t + 1, ts - 1), 1)
        base = pl.multiple_of(t * SCAN_IHI, SCAN_IHI)
        v = vs_ref[pl.ds(base, SCAN_IHI), :]
        yparts = [None] * n_acc
        sparts = [None] * n_acc
        for j in range(HEAD_DIM):
            sj = s_ref[j] * jt_ref[0, j, row, :] + sa * jt_ref[2, j, row, :] + v * jt_ref[3, j, row, :]
            s_ref[j] = sj
            add_part(yparts, j, sj * jt_ref[4, j, row, :])
            add_part(sparts, j, sj * jt_ref[1, j, nxt, :])
        ys_ref[pl.ds(base, SCAN_IHI), :] = tree_sum(yparts)
        return tree_sum(sparts)

    first = [None] * n_acc
    for j in range(HEAD_DIM):
        add_part(first, j, s_ref[j] * jt_ref[1, j, pl.ds(0, 1), :])
    lax.fori_loop(0, ts, step, tree_sum(first), unroll=2)

    for ihi in range(SCAN_IHI):
        yt = ys_ref[pl.ds(ihi, ts, stride=SCAN_IHI), :].T
        for q in range(nq):
            for ilo in range(SCAN_ILO):
                r0 = (q * SCAN_ILO + ilo) * RWKV_HEADS
                xt_ref[q, pl.ds(ihi * SCAN_ILO + ilo, RWKV_HEADS, stride=HEAD_DIM), :] = yt[r0:r0 + RWKV_HEADS]
    per_dir = nq // N_DIR
    for q in range(nq):
        y = xt_ref[q].T
        y_refs[q // per_dir][q % per_dir] = flip(y) if backward(q) else y


def _rwkv_scan(jv, misc, batch, seq, ctx_len, ts):
    nq = N_DIR * batch
    assert nq * RWKV_HEADS * SCAN_ILO == V7X_LANES and seq % ts == 0 and ctx_len % ts == 0
    n_l, n_c = seq // ts, ctx_len // ts
    W = RWKV_WIDTH

    def row_block(d, b):
        def f(i):
            is_ctx = i < n_c
            k_ctx = (n_c - 1 - i) if d else i
            k_lat = (n_l - 1 - (i - n_c)) if d else (i - n_c)
            return jnp.where(is_ctx, batch * n_l + b * n_c + k_ctx, b * n_l + k_lat)
        return f

    def pos_block(d):
        def f(i):
            return jnp.where(i < n_c, n_c - 1 - i, n_c + n_l - 1 - (i - n_c)) if d else i
        return f

    chains = [(d, b) for d in range(N_DIR) for b in range(batch)]
    j_specs = [pl.BlockSpec((ts, N_JVEC * W), lambda i, d=d, f=row_block(d, b): (f(i), d)) for d, b in chains]
    v_specs = [pl.BlockSpec((ts, W), lambda i, f=row_block(d, b): (f(i), 2)) for d, b in chains]
    y_specs = [pl.BlockSpec((batch, ts, W), lambda i, f=pos_block(d): (0, f(i), 0)) for d in range(N_DIR)]
    rev = jnp.asarray(np.eye(ts, dtype=np.float32)[::-1].copy(), BF16)
    return pl.pallas_call(
        functools.partial(_scan_body, ts=ts, n_chain_grp=nq),
        grid=(n_c + n_l,),
        in_specs=[pl.BlockSpec((ts, ts), lambda i: (0, 0))] + j_specs + v_specs,
        out_specs=y_specs,
        out_shape=[jax.ShapeDtypeStruct((batch, seq + ctx_len, W), F32)] * N_DIR,
        scratch_shapes=[pltpu.VMEM((HEAD_DIM, SCAN_IHI, V7X_LANES), F32),
                        pltpu.VMEM((N_JVEC, HEAD_DIM, ts, V7X_LANES), F32),
                        pltpu.VMEM((ts * SCAN_IHI, V7X_LANES), F32),
                        pltpu.VMEM((ts * SCAN_IHI, V7X_LANES), F32),
                        pltpu.VMEM((nq, W, ts), F32)],
        compiler_params=_cparams(("arbitrary",)),
        name="scan",
    )(rev, *([jv] * nq), *([misc] * nq))


def _rwkvout_body(yf_ref, yb_ref, misc_ref, rk_ref, lg_ref, lb_ref, dmean_ref, dsum_ref, up_ref, o_ref):
    W = RWKV_WIDTH
    y = yf_ref[0] + yb_ref[0]
    r = misc_ref[:, 0:W]
    k = misc_ref[:, W:2 * W]
    v = misc_ref[:, 2 * W:3 * W]
    g = misc_ref[:, 3 * W:4 * W]
    mean = _seg_reduce(y, dmean_ref[...], up_ref[...])
    yc = y - mean
    var = _seg_reduce(yc * yc, dmean_ref[...], up_ref[...])
    yn = yc * lax.rsqrt(var + GN_EPS) * lg_ref[...] + lb_ref[...]
    bonus = _seg_reduce(r * k * rk_ref[...], dsum_ref[...], up_ref[...]) * v
    o_ref[...] = (yn + bonus) * g


def _rwkvout(y2, misc, r_k, ln_g, ln_b, tm, n_lat, seq, ctx_len):
    m = misc.shape[0]

    def ypos(i):
        r0 = i * tm
        in_lat = r0 < n_lat
        b = jnp.where(in_lat, r0 // seq, (r0 - n_lat) // ctx_len)
        pos = jnp.where(in_lat, ctx_len + r0 % seq, (r0 - n_lat) % ctx_len)
        return (b, pos // tm, 0)

    dmean, up = _seg_matrices(RWKV_WIDTH, mean=True)
    dsum, _ = _seg_matrices(RWKV_WIDTH, mean=False)
    full = lambda a: pl.BlockSpec(a.shape, lambda i: (0,) * a.ndim)
    params = [r_k, ln_g, ln_b, dmean, dsum, up]
    return pl.pallas_call(
        _rwkvout_body,
        grid=(m // tm,),
        in_specs=[pl.BlockSpec((1, tm, RWKV_WIDTH), ypos), pl.BlockSpec((1, tm, RWKV_WIDTH), ypos),
                  pl.BlockSpec((tm, 4 * RWKV_WIDTH), lambda i: (i, 0))] + [full(a) for a in params],
        out_specs=pl.BlockSpec((tm, RWKV_WIDTH), lambda i: (i, 0)),
        out_shape=jax.ShapeDtypeStruct((m, RWKV_WIDTH), F32),
        compiler_params=_cparams(("parallel",)),
        name="rwkvout",
    )(y2[0], y2[1], misc, *params)


def _conv_body(p_ref, pv_ref, nx_ref, w_ref, b_ref, g_ref, o_ref, ext_ref, *, tm, n_lat, seq, ctx_len):
    first, last = _seq_edges(pl.program_id(0), tm, n_lat, seq, ctx_len)
    C = CONV_WIDTH
    H = CONV_HALO

    def glu(t):
        return t[:, :C] * jax.nn.sigmoid(t[:, C:])

    ext_ref[H:H + tm, :] = glu(p_ref[...])
    ext_ref[0:H, :] = jnp.where(first, 0.0, glu(pv_ref[...]))
    ext_ref[H + tm:H + tm + H, :] = jnp.where(last, 0.0, glu(nx_ref[...]))
    rows = 64
    for c in range(tm // rows):
        acc = jnp.zeros((rows, C), F32)
        for t in range(CONV_KSIZE):
            off = c * rows + H - CONV_PAD + t
            acc = acc + ext_ref[off:off + rows, :] * w_ref[t:t + 1, :]
        u = acc + b_ref[...]
        ms = jnp.mean(u * u, axis=-1, keepdims=True)
        z = u * lax.rsqrt(ms + EPS) * g_ref[...]
        o_ref[c * rows:(c + 1) * rows, :] = z * jax.nn.sigmoid(z)


def _conv(p_cv, w, b, g, tm, n_lat, seq, ctx_len):
    m = p_cv.shape[0]
    prev, nxt = _halo_specs(tm, CONV_HALO, 2 * CONV_WIDTH, m)
    full = lambda a: pl.BlockSpec(a.shape, lambda i: (0,) * a.ndim)
    return pl.pallas_call(
        functools.partial(_conv_body, tm=tm, n_lat=n_lat, seq=seq, ctx_len=ctx_len),
        grid=(m // tm,),
        in_specs=[pl.BlockSpec((tm, 2 * CONV_WIDTH), lambda i: (i, 0)), prev, nxt, full(w), full(b), full(g)],
        out_specs=pl.BlockSpec((tm, CONV_WIDTH), lambda i: (i, 0)),
        out_shape=jax.ShapeDtypeStruct((m, CONV_WIDTH), F32),
        scratch_shapes=[pltpu.VMEM((tm + 2 * CONV_HALO, CONV_WIDTH), F32)],
        compiler_params=_cparams(("parallel",)),
        name="conv",
    )(p_cv, p_cv, p_cv, w, b, g)


def _outproj_body(ya_ref, yr_ref, yc_ref, x_ref, g_ref, wa_ref, wr_ref, wc_ref, o_ref):
    acc = jnp.dot(ya_ref[...].astype(BF16), wa_ref[...], preferred_element_type=F32)
    acc += jnp.dot(yr_ref[...].astype(BF16), wr_ref[...], preferred_element_type=F32)
    acc += jnp.dot(yc_ref[...].astype(BF16), wc_ref[...], preferred_element_type=F32)
    o_ref[...] = x_ref[...] + g_ref[0] * acc


def _outproj(ya, yr, yc, x, gate, w, n_lat, seq, tm, tn):
    m, d = x.shape
    n_grp = gate.shape[0] - 1

    def grp(i, j):
        r0 = i * tm
        return (jnp.where(r0 < n_lat, r0 // seq, n_grp), 0, j)

    wa, wr, wc = w[:ATT_WIDTH], w[ATT_WIDTH:ATT_WIDTH + RWKV_WIDTH], w[ATT_WIDTH + RWKV_WIDTH:]
    return pl.pallas_call(
        _outproj_body,
        grid=(m // tm, d // tn),
        in_specs=[pl.BlockSpec((tm, ATT_WIDTH), lambda i, j: (i, 0)),
                  pl.BlockSpec((tm, RWKV_WIDTH), lambda i, j: (i, 0)),
                  pl.BlockSpec((tm, CONV_WIDTH), lambda i, j: (i, 0)),
                  pl.BlockSpec((tm, tn), lambda i, j: (i, j)),
                  pl.BlockSpec((1, 1, tn), grp),
                  pl.BlockSpec((ATT_WIDTH, tn), lambda i, j: (0, j)),
                  pl.BlockSpec((RWKV_WIDTH, tn), lambda i, j: (0, j)),
                  pl.BlockSpec((CONV_WIDTH, tn), lambda i, j: (0, j))],
        out_specs=pl.BlockSpec((tm, tn), lambda i, j: (i, j)),
        out_shape=jax.ShapeDtypeStruct((m, d), F32),
        compiler_params=_cparams(("parallel", "arbitrary")),
        name="outproj",
    )(ya, yr, yc, x, gate, wa, wr, wc)


def _peer_pairs():
    n = PEER_TOPK + 1
    return [(a, b) for a in range(n) for b in range(n) if (a + 1) * (b + 1) <= n]


def _top_distinct(s, n, count):
    vals, cnts = [], []
    v = s
    for _ in range(n):
        mx = jnp.max(v, axis=0, keepdims=True)
        hit = v == mx
        vals.append(mx)
        if count:
            cnt = jnp.sum(hit.astype(F32), axis=0, keepdims=True)
            cnts.append(jnp.where(mx == -jnp.inf, 0.0, cnt))
        v = jnp.where(hit, -jnp.inf, v)
    if count:
        return vals, cnts
    return vals, jnp.sum((v == -jnp.inf).astype(F32), axis=0, keepdims=True)


def _peer_head_factors(q_ref, keys_ref, h, count):
    n = PEER_TOPK + 1
    nt = (((1,), (1,)), ((), ()))
    pairs = _peer_pairs()
    sc = []
    for p in range(2):
        c0 = (h * 2 + p) * PEER_N_KEYS
        sc.append(lax.dot_general(keys_ref[h, p], q_ref[:, c0:c0 + PEER_N_KEYS], nt,
                                  preferred_element_type=F32, precision=HIGHEST))
    (v1, n1), (v2, n2) = _top_distinct(sc[0], n, count), _top_distinct(sc[1], n, count)
    cv = jnp.concatenate([v1[a] + v2[b] for a, b in pairs], axis=0)
    cm = jnp.concatenate([n1[a] * n2[b] for a, b in pairs], axis=0) if count else None
    top = v1[0] + v2[0]
    cum = jnp.zeros_like(top)
    t16 = jnp.full_like(top, -jnp.inf)
    t17 = jnp.full_like(top, -jnp.inf)
    z = jnp.zeros_like(top)
    for _ in range(n):
        mx = jnp.max(cv, axis=0, keepdims=True)
        hit = cv == mx
        hits = jnp.where(hit, cm, 0.0) if count else hit.astype(F32)
        cnt = jnp.where(mx == -jnp.inf, 0.0, jnp.sum(hits, axis=0, keepdims=True))
        new = cum + cnt
        t16 = jnp.where((cum < PEER_TOPK) & (new >= PEER_TOPK), mx, t16)
        t17 = jnp.where((cum < n) & (new >= n), mx, t17)
        take = jnp.minimum(cnt, jnp.maximum(PEER_TOPK - cum, 0.0))
        z = z + jnp.where(take > 0, take * jnp.exp(mx - top), 0.0)
        cum = new
        cv = jnp.where(hit, -jnp.inf, cv)
    thr = 0.5 * (t16 + t17)
    factors = (jnp.exp(sc[1] - v2[0]), jnp.exp(thr - sc[0] - v2[0]), jnp.exp(sc[0] - v1[0]) * (0.5 / z))
    if count:
        return factors, None
    return factors, (n1 != n).astype(F32) + (n2 != n).astype(F32)


def _peertopk_body(q_ref, keys_ref, e2_ref, a1_ref, e1_ref):
    dup = None
    for h in range(PEER_HEADS):
        (e2_ref[h], a1_ref[h], e1_ref[h]), d = _peer_head_factors(q_ref, keys_ref, h, count=False)
        dup = d if dup is None else dup + d

    @pl.when(jnp.max(dup) > 0)
    def _():
        for h in range(PEER_HEADS):
            (e2_ref[h], a1_ref[h], e1_ref[h]), _ = _peer_head_factors(q_ref, keys_ref, h, count=True)


def _peertopk(q, keys, tt):
    m, qw = q.shape
    shp = jax.ShapeDtypeStruct((PEER_HEADS, PEER_N_KEYS, m), F32)
    ospec = pl.BlockSpec((PEER_HEADS, PEER_N_KEYS, tt), lambda i: (0, 0, i))
    return pl.pallas_call(
        _peertopk_body,
        grid=(m // tt,),
        in_specs=[pl.BlockSpec((tt, qw), lambda i: (i, 0)),
                  pl.BlockSpec(keys.shape, lambda i: (0, 0, 0, 0))],
        out_specs=[ospec] * 3,
        out_shape=[shp] * 3,
        compiler_params=_cparams(("parallel",)),
        name="peertopk",
    )(q, keys)


def _peerdense_body(h_ref, u_ref, vt_ref, e2_ref, a1_ref, e1_ref, x_ref, g_ref, o_ref, acc_ref, a_ref, ht_ref,
                    st_ref, *, ec):
    e = pl.program_id(1)
    tt = ht_ref.shape[1]

    @pl.when(e == 0)
    def _():
        acc_ref[...] = jnp.zeros_like(acc_ref)
        ht_ref[...] = h_ref[...].astype(F32).T.astype(BF16)

    tc = 2 * V7X_LANES
    rh = PEER_N_KEYS // 2
    piece = 2 * PEER_N_KEYS
    for p0 in range(0, ec, piece):
        st_ref[p0:p0 + piece, :] = jnp.dot(u_ref[p0:p0 + piece, :], ht_ref[...], preferred_element_type=F32)
        for ii in range(p0 // PEER_N_KEYS, (p0 + piece) // PEER_N_KEYS):
            for c in range(tt // tc):
                cs = pl.ds(c * tc, tc)
                for r in range(PEER_N_KEYS // rh):
                    ks = pl.ds(r * rh, rh)
                    w = None
                    for h in range(PEER_HEADS):
                        e2 = e2_ref[h, ks, cs]
                        t = jnp.where(e2 >= a1_ref[h, 0, pl.ds(ii, 1), cs], e2, 0.0) * e1_ref[h, 0, pl.ds(ii, 1), cs]
                        w = t if w is None else w + t
                    rows = pl.ds(ii * PEER_N_KEYS + r * rh, rh)
                    s = st_ref[rows, cs]
                    act2 = s * (1.0 + lax.erf(s * np.float32(1.0 / np.sqrt(2.0))))
                    a_ref[rows, cs] = (act2 * w).astype(BF16)
    acc_ref[...] += jnp.dot(vt_ref[...], a_ref[...], preferred_element_type=F32)

    @pl.when(e == pl.num_programs(1) - 1)
    def _():
        o_ref[...] = x_ref[...] + g_ref[0] * acc_ref[...].T


def _peerdense(h2, u, vt, e2, a1, e1, x, gate, n_lat, seq, tt, ec, m_out):
    d = x.shape[1]
    m = m_out
    n_chunks = u.shape[0] // ec
    n_grp = gate.shape[0] - 1

    def grp(i, e):
        r0 = i * tt
        return (jnp.where(r0 < n_lat, r0 // seq, n_grp), 0, 0)

    per = ec // PEER_N_KEYS
    a1 = a1.reshape(PEER_HEADS, n_chunks, per, a1.shape[2])
    e1 = e1.reshape(PEER_HEADS, n_chunks, per, e1.shape[2])
    row_spec = pl.BlockSpec((PEER_HEADS, 1, per, tt), lambda i, e: (0, e, 0, i))
    return pl.pallas_call(
        functools.partial(_peerdense_body, ec=ec),
        grid=(m // tt, n_chunks),
        in_specs=[pl.BlockSpec((tt, d), lambda i, e: (i, 0)),
                  pl.BlockSpec((ec, d), lambda i, e: (e, 0)),
                  pl.BlockSpec((d, ec), lambda i, e: (0, e)),
                  pl.BlockSpec((PEER_HEADS, PEER_N_KEYS, tt), lambda i, e: (0, 0, i)),
                  row_spec, row_spec,
                  pl.BlockSpec((tt, d), lambda i, e: (i, 0)),
                  pl.BlockSpec((1, 1, d), grp)],
        out_specs=pl.BlockSpec((tt, d), lambda i, e: (i, 0)),
        out_shape=jax.ShapeDtypeStruct((m, d), F32),
        scratch_shapes=[pltpu.VMEM((d, tt), F32), pltpu.VMEM((ec, tt), BF16), pltpu.VMEM((d, tt), BF16),
                        pltpu.VMEM((ec, tt), F32)],
        compiler_params=_cparams(("parallel", "arbitrary")),
        name="peerdense",
    )(h2, u, vt, e2, a1, e1, x, gate)


def _tiles(seq, ctx_len, batch):
    ctx_rows = batch * ctx_len
    return dict(
        mm=min(512, seq, ctx_rows),
        seqt=min(256, seq, ctx_len),
        peer=min(512, seq, ctx_rows),
        scan=min(128, seq, ctx_len),
    )


def _rope_tables(batch, seq, ctx_len):
    inv = ROPE_THETA ** (-jnp.arange(0, ROPE_AXIS_DIM, 2, dtype=F32) / ROPE_AXIS_DIM)
    t = jnp.arange(seq, dtype=jnp.int32)
    ang_r = (t // GRID_W).astype(F32)[:, None] * inv[None, :]
    ang_c = (t % GRID_W).astype(F32)[:, None] * inv[None, :]
    cos = jnp.concatenate([jnp.cos(ang_r)] * 2 + [jnp.cos(ang_c)] * 2, axis=-1)
    sin = jnp.concatenate([-jnp.sin(ang_r), jnp.sin(ang_r), -jnp.sin(ang_c), jnp.sin(ang_c)], axis=-1)
    n_ctx = batch * ctx_len
    cos = jnp.concatenate([jnp.tile(cos, (batch, 1)), jnp.ones((n_ctx, HEAD_DIM), F32)], axis=0)
    sin = jnp.concatenate([jnp.tile(sin, (batch, 1)), jnp.zeros((n_ctx, HEAD_DIM), F32)], axis=0)
    return jnp.tile(cos, (1, 2)), jnp.tile(sin, (1, 2))


def _block_diag2(w):
    z = jnp.zeros_like(w[0])
    return jnp.concatenate([jnp.concatenate([w[0], z], axis=1), jnp.concatenate([z, w[1]], axis=1)], axis=0)


def kernel(x, c, ctx, c_ctx, norm1_g, norm2_g, w_ada, b_ada, w_in, w_out, q_norm_g, k_norm_g, attn_sink, rwkv_mu, rwkv_w0, rwkv_w_up, rwkv_a0, rwkv_a_up, rwkv_g_up, rwkv_k_k, rwkv_k_a, rwkv_r_k, rwkv_ln_g, rwkv_ln_b, conv_w, conv_b, conv_norm_g, peer_wq, peer_keys, peer_u, peer_v):
    batch, seq, d = x.shape
    ctx_len = ctx.shape[1]
    depth = w_in.shape[0]
    n_lat = batch * seq
    tl = _tiles(seq, ctx_len, batch)
    assert seq % ATT_BLOCK == 0 and ctx_len % ATT_BLOCK == 0 and n_lat % ctx_len == 0
    assert (seq + ctx_len) % tl["scan"] == 0

    xt = jnp.concatenate([x.reshape(n_lat, d), ctx.reshape(batch * ctx_len, d)], axis=0)
    cc = jnp.concatenate([c, c_ctx[None, :], jnp.zeros((8 - batch - 1, d), F32)], axis=0)
    cos2, sin2 = _rope_tables(batch, seq, ctx_len)
    row = lambda a: a.reshape(1, -1)

    for i in range(depth):
        mod = _ada(cc, w_ada[i], row(b_ada[i]))[:batch + 1].reshape(batch + 1, 6, 1, d)
        sh1, sc1, g1, sh2, sc2, g2 = (mod[:, k] for k in range(6))

        w_in_b = w_in[i].astype(BF16)
        p_att, h1 = _modmm(xt, row(norm1_g[i]), sc1, sh1, w_in_b[:, :ATT_COLS], n_lat, seq, tl["mm"], ATT_COLS)
        p_rw = _mm(h1, w_in_b[:, ATT_COLS:ATT_COLS + RWKV_COLS], tl["mm"], RWKV_COLS)
        p_cv = _mm(h1, w_in_b[:, ATT_COLS + RWKV_COLS:], tl["mm"], 2 * CONV_WIDTH)

        gain = jnp.concatenate([jnp.tile(q_norm_g[i] * HEAD_DIM ** -0.5, ATT_HEADS), jnp.tile(k_norm_g[i], ATT_KV_HEADS)])
        qk, vv = _qkprep(p_att, row(gain), cos2, sin2, tl["seqt"])
        y_att = _attention(qk, vv, attn_sink[i], batch, seq, ctx_len)

        jv, misc = _rwkvfeat(p_rw, row(rwkv_mu[i]), row(rwkv_w0[i]), _block_diag2(rwkv_w_up[i]),
                             row(rwkv_a0[i]), _block_diag2(rwkv_a_up[i]), rwkv_g_up[i],
                             row(rwkv_k_k[i]), row(rwkv_k_a[i]), tl["seqt"], n_lat, seq, ctx_len)
        y2 = _rwkv_scan(jv, misc, batch, seq, ctx_len, tl["scan"])
        y_rw = _rwkvout(y2, misc, row(rwkv_r_k[i]), row(rwkv_ln_g[i]), row(rwkv_ln_b[i]), tl["seqt"],
                        n_lat, seq, ctx_len)

        y_cv = _conv(p_cv, conv_w[i], row(conv_b[i]), row(conv_norm_g[i]), tl["seqt"], n_lat, seq, ctx_len)

        xt = _outproj(y_att, y_rw, y_cv, xt, g1, w_out[i].astype(BF16), n_lat, seq, tl["mm"], d)

        q, h2 = _modmm(xt, row(norm2_g[i]), sc2, sh2, peer_wq[i].astype(BF16), n_lat, seq, tl["mm"], peer_wq.shape[2])
        e2, a1, e1 = _peertopk(q, peer_keys[i], tl["peer"])
        xt = _peerdense(h2, peer_u[i].astype(BF16), peer_v[i].T.astype(BF16), e2, a1, e1, xt, g2,
                        n_lat, seq, tl["peer"], 1024, n_lat if i == depth - 1 else xt.shape[0])
    return xt.reshape(batch, seq, d)
```

```python
import functools

import jax
import jax.numpy as jnp
import numpy as np
from jax import lax
from jax.experimental import pallas as pl
from jax.experimental.pallas import tpu as pltpu

F32 = jnp.float32
BF16 = jnp.bfloat16
HIGHEST = lax.Precision.HIGHEST

V7X_LANES = 128
V7X_SUBLANES = 8
V7X_VMEM_BYTES = 64 * 1024 * 1024
VMEM_LIMIT = V7X_VMEM_BYTES - 8 * 1024 * 1024

HEAD_DIM = 64
EPS = 1e-6
GN_EPS = 64e-5
NEG_INF = -1e30
GRID_W = 64
ATT_HEADS = 16
ATT_KV_HEADS = 4
ATT_GROUP = ATT_HEADS // ATT_KV_HEADS
ATT_WIDTH = ATT_HEADS * HEAD_DIM
KV_WIDTH = ATT_KV_HEADS * HEAD_DIM
QK_WIDTH = ATT_WIDTH + KV_WIDTH
ATT_COLS = ATT_WIDTH + 2 * KV_WIDTH
ATT_BLOCK = 128
ROPE_THETA = 10000.0
ROPE_AXIS_DIM = HEAD_DIM // 2
ROPE_HALF = ROPE_AXIS_DIM // 2
RWKV_HEADS = 8
RWKV_WIDTH = RWKV_HEADS * HEAD_DIM
W_LORA = 64
A_LORA = 64
G_LORA = 128
N_DIR = 2
RWKV_COLS = 3 * RWKV_WIDTH + N_DIR * W_LORA + N_DIR * A_LORA + G_LORA
CONV_WIDTH = 512
CONV_KSIZE = 31
CONV_PAD = (CONV_KSIZE - 1) // 2
CONV_HALO = 16
PEER_HEADS = 8
PEER_N_KEYS = 128
PEER_TOPK = 16
N_JVEC = 5
SCAN_ILO = 4
SCAN_IHI = HEAD_DIM // SCAN_ILO


def _cparams(sem):
    return pltpu.CompilerParams(dimension_semantics=sem, vmem_limit_bytes=VMEM_LIMIT)


def _seg_matrices(width, mean):
    nseg = width // HEAD_DIM
    seg = np.arange(width) // HEAD_DIM
    down = np.zeros((width, V7X_LANES), np.float32)
    down[np.arange(width), seg] = 1.0 / HEAD_DIM if mean else 1.0
    up = np.zeros((V7X_LANES, width), np.float32)
    up[seg, np.arange(width)] = 1.0
    assert nseg <= V7X_LANES
    return jnp.asarray(down, BF16), jnp.asarray(up, BF16)


def _split2(a):
    hi = a.astype(BF16)
    return hi, (a - hi.astype(F32)).astype(BF16)


def _dot_exact_rhs(a, m):
    hi, lo = _split2(a)
    return jnp.dot(hi, m, preferred_element_type=F32) + jnp.dot(lo, m, preferred_element_type=F32)


def _dot3(a, b):
    a_hi, a_lo = _split2(a)
    b_hi, b_lo = _split2(b)
    return (jnp.dot(a_hi, b_hi, preferred_element_type=F32) + jnp.dot(a_lo, b_hi, preferred_element_type=F32)
            + jnp.dot(a_hi, b_lo, preferred_element_type=F32))


def _seg_reduce(x, down, up):
    return _dot_exact_rhs(_dot_exact_rhs(x, down), up)


def _ada_body(c_ref, w_ref, b_ref, o_ref):
    cc = c_ref[...]
    s = (cc * jax.nn.sigmoid(cc)).astype(BF16)
    o_ref[...] = jnp.dot(s, w_ref[...].astype(BF16), preferred_element_type=F32) + b_ref[...]


def _ada(cc, w, b):
    d, n = w.shape
    tn = 1024
    return pl.pallas_call(
        _ada_body,
        grid=(n // tn,),
        in_specs=[pl.BlockSpec((8, d), lambda j: (0, 0)),
                  pl.BlockSpec((d, tn), lambda j: (0, j)),
                  pl.BlockSpec((1, tn), lambda j: (0, j))],
        out_specs=pl.BlockSpec((8, tn), lambda j: (0, j)),
        out_shape=jax.ShapeDtypeStruct((8, n), F32),
        compiler_params=_cparams(("arbitrary",)),
        name="ada",
    )(cc, w, b)


def _modmm_body(x_ref, g_ref, sc_ref, sh_ref, w_ref, o_ref, h_ref):
    @pl.when(pl.program_id(1) == 0)
    def _():
        x = x_ref[...]
        ms = jnp.mean(x * x, axis=-1, keepdims=True)
        y = x * lax.rsqrt(ms + EPS) * g_ref[...]
        h_ref[...] = (y * (1.0 + sc_ref[0]) + sh_ref[0]).astype(BF16)

    o_ref[...] = jnp.dot(h_ref[...], w_ref[...], preferred_element_type=F32)


def _modmm(x, g, sc, sh, w, n_lat, seq, tm, tn):
    m, d = x.shape
    n = w.shape[1]
    n_grp = sc.shape[0] - 1

    def grp(i, j):
        r0 = i * tm
        return (jnp.where(r0 < n_lat, r0 // seq, n_grp), 0, 0)

    return pl.pallas_call(
        _modmm_body,
        grid=(m // tm, n // tn),
        in_specs=[pl.BlockSpec((tm, d), lambda i, j: (i, 0)),
                  pl.BlockSpec((1, d), lambda i, j: (0, 0)),
                  pl.BlockSpec((1, 1, d), grp),
                  pl.BlockSpec((1, 1, d), grp),
                  pl.BlockSpec((d, tn), lambda i, j: (0, j))],
        out_specs=[pl.BlockSpec((tm, tn), lambda i, j: (i, j)),
                   pl.BlockSpec((tm, d), lambda i, j: (i, 0))],
        out_shape=[jax.ShapeDtypeStruct((m, n), F32), jax.ShapeDtypeStruct((m, d), BF16)],
        compiler_params=_cparams(("parallel", "arbitrary")),
        name="modmm",
    )(x, g, sc, sh, w)


def _mm_body(h_ref, w_ref, o_ref):
    o_ref[...] = jnp.dot(h_ref[...], w_ref[...], preferred_element_type=F32)


def _mm(h, w, tm, tn):
    m, d = h.shape
    n = w.shape[1]
    return pl.pallas_call(
        _mm_body,
        grid=(m // tm, n // tn),
        in_specs=[pl.BlockSpec((tm, d), lambda i, j: (i, 0)),
                  pl.BlockSpec((d, tn), lambda i, j: (0, j))],
        out_specs=pl.BlockSpec((tm, tn), lambda i, j: (i, j)),
        out_shape=jax.ShapeDtypeStruct((m, n), F32),
        compiler_params=_cparams(("parallel", "arbitrary")),
        name="mm",
    )(h, w)


def _qkprep_body(p_ref, gain_ref, cos_ref, sin_ref, down_ref, up_ref, qk_ref, v_ref):
    x = p_ref[:, :QK_WIDTH]
    ms = _seg_reduce(x * x, down_ref[...], up_ref[...])
    y = x * lax.rsqrt(ms + EPS) * gain_ref[...]
    cos = cos_ref[...]
    sin = sin_ref[...]
    lane = lax.broadcasted_iota(jnp.int32, cos.shape, 1)
    first = (lane % ROPE_AXIS_DIM) < ROPE_HALF
    for blk in range(QK_WIDTH // V7X_LANES):
        yb = y[:, blk * V7X_LANES:(blk + 1) * V7X_LANES]
        partner = jnp.where(first, pltpu.roll(yb, V7X_LANES - ROPE_HALF, 1), pltpu.roll(yb, ROPE_HALF, 1))
        qk_ref[:, blk * V7X_LANES:(blk + 1) * V7X_LANES] = (yb * cos + partner * sin).astype(BF16)
    v_ref[...] = p_ref[:, QK_WIDTH:].astype(BF16)


def _qkprep(p_att, gain, cos2, sin2, tm):
    m = p_att.shape[0]
    down, up = _seg_matrices(QK_WIDTH, mean=True)
    return pl.pallas_call(
        _qkprep_body,
        grid=(m // tm,),
        in_specs=[pl.BlockSpec((tm, ATT_COLS), lambda i: (i, 0)),
                  pl.BlockSpec((1, QK_WIDTH), lambda i: (0, 0)),
                  pl.BlockSpec((tm, V7X_LANES), lambda i: (i, 0)),
                  pl.BlockSpec((tm, V7X_LANES), lambda i: (i, 0)),
                  pl.BlockSpec(down.shape, lambda i: (0, 0)),
                  pl.BlockSpec(up.shape, lambda i: (0, 0))],
        out_specs=[pl.BlockSpec((tm, QK_WIDTH), lambda i: (i, 0)),
                   pl.BlockSpec((tm, KV_WIDTH), lambda i: (i, 0))],
        out_shape=[jax.ShapeDtypeStruct((m, QK_WIDTH), BF16), jax.ShapeDtypeStruct((m, KV_WIDTH), BF16)],
        compiler_params=_cparams(("parallel",)),
        name="qkprep",
    )(p_att, gain, cos2, sin2, down, up)


def _attn_body(sink_ref, q_ref, k0_ref, k1_ref, k2_ref, v0_ref, v1_ref, v2_ref, kc_ref, vc_ref, o_ref, *, nb):
    n = pl.program_id(1)
    is_lat = n < nb
    rows = ATT_GROUP * ATT_BLOCK
    qi = lax.broadcasted_iota(jnp.int32, (rows, 3 * ATT_BLOCK), 0) % ATT_BLOCK
    si = lax.broadcasted_iota(jnp.int32, (rows, 3 * ATT_BLOCK), 1)
    kblk = n - 1 + si // ATT_BLOCK
    mask = (jnp.abs(si - ATT_BLOCK - qi) <= ATT_BLOCK) & (kblk >= 0) & (kblk < nb) & is_lat
    nt = (((1,), (1,)), ((), ()))
    for g in range(ATT_KV_HEADS):
        ks = slice(g * HEAD_DIM, (g + 1) * HEAD_DIM)
        qs = jnp.concatenate(
            [q_ref[:, (g * ATT_GROUP + a) * HEAD_DIM:(g * ATT_GROUP + a + 1) * HEAD_DIM] for a in range(ATT_GROUP)],
            axis=0)
        kw = jnp.concatenate([k0_ref[:, ks], k1_ref[:, ks], k2_ref[:, ks]], axis=0)
        vw = jnp.concatenate([v0_ref[:, ks], v1_ref[:, ks], v2_ref[:, ks]], axis=0)
        s_win = jnp.where(mask, lax.dot_general(qs, kw, nt, preferred_element_type=F32), NEG_INF)
        s_ctx = lax.dot_general(qs, kc_ref[:, ks], nt, preferred_element_type=F32)
        row_head = lax.broadcasted_iota(jnp.int32, (rows, 1), 0) // ATT_BLOCK
        snk = jnp.zeros((rows, 1), F32)
        for a in range(ATT_GROUP):
            snk = jnp.where(row_head == a, sink_ref[g * ATT_GROUP + a], snk)
        mx = jnp.maximum(jnp.maximum(jnp.max(s_win, axis=-1, keepdims=True),
                                     jnp.max(s_ctx, axis=-1, keepdims=True)), snk)
        p_win = jnp.exp(s_win - mx)
        p_ctx = jnp.exp(s_ctx - mx)
        den = (jnp.sum(p_win, axis=-1, keepdims=True) + jnp.sum(p_ctx, axis=-1, keepdims=True)
               + jnp.exp(snk - mx))
        acc = (jnp.dot(p_win.astype(BF16), vw, preferred_element_type=F32)
               + jnp.dot(p_ctx.astype(BF16), vc_ref[:, ks], preferred_element_type=F32))
        out = acc / den
        for a in range(ATT_GROUP):
            h = g * ATT_GROUP + a
            o_ref[:, h * HEAD_DIM:(h + 1) * HEAD_DIM] = out[a * ATT_BLOCK:(a + 1) * ATT_BLOCK]


def _attention(qk, v, sink, batch, seq, ctx_len):
    m = qk.shape[0]
    nb = seq // ATT_BLOCK
    ncb = ctx_len // ATT_BLOCK
    ctx_blk0 = batch * nb
    kcol = ATT_WIDTH // KV_WIDTH

    def qrow(b, n):
        return jnp.where(n < nb, b * nb + n, ctx_blk0 + b * ncb + (n - nb))

    def krow(off):
        def f(b, n, *_):
            return (b * nb + jnp.clip(n + off, 0, nb - 1), kcol)
        return f

    def vrow(off):
        def f(b, n, *_):
            return (b * nb + jnp.clip(n + off, 0, nb - 1), 0)
        return f

    ctx_row = (batch * seq) // ctx_len
    grid_spec = pltpu.PrefetchScalarGridSpec(
        num_scalar_prefetch=1,
        grid=(batch, nb + ncb),
        in_specs=[pl.BlockSpec((ATT_BLOCK, ATT_WIDTH), lambda b, n, *_: (qrow(b, n), 0)),
                  pl.BlockSpec((ATT_BLOCK, KV_WIDTH), krow(-1)),
                  pl.BlockSpec((ATT_BLOCK, KV_WIDTH), krow(0)),
                  pl.BlockSpec((ATT_BLOCK, KV_WIDTH), krow(1)),
                  pl.BlockSpec((ATT_BLOCK, KV_WIDTH), vrow(-1)),
                  pl.BlockSpec((ATT_BLOCK, KV_WIDTH), vrow(0)),
                  pl.BlockSpec((ATT_BLOCK, KV_WIDTH), vrow(1)),
                  pl.BlockSpec((ctx_len, KV_WIDTH), lambda b, n, *_: (ctx_row + b, kcol)),
                  pl.BlockSpec((ctx_len, KV_WIDTH), lambda b, n, *_: (ctx_row + b, 0))],
        out_specs=pl.BlockSpec((ATT_BLOCK, ATT_WIDTH), lambda b, n, *_: (qrow(b, n), 0)),
    )
    return pl.pallas_call(
        functools.partial(_attn_body, nb=nb),
        grid_spec=grid_spec,
        out_shape=jax.ShapeDtypeStruct((m, ATT_WIDTH), F32),
        compiler_params=_cparams(("parallel", "arbitrary")),
        name="attention",
    )(sink, qk, qk, qk, qk, v, v, v, qk, v)


def _seq_edges(i, tm, n_lat, seq, ctx_len):
    r0 = i * tm
    in_lat = r0 < n_lat
    pos = jnp.where(in_lat, r0 % seq, (r0 - n_lat) % ctx_len)
    length = jnp.where(in_lat, seq, ctx_len)
    return pos == 0, pos + tm == length


def _halo_specs(tm, halo, width, n_rows):
    per = tm // halo
    last = n_rows // halo - 1
    prev = pl.BlockSpec((halo, width), lambda i: (jnp.maximum(i * per - 1, 0), 0))
    nxt = pl.BlockSpec((halo, width), lambda i: (jnp.minimum((i + 1) * per, last), 0))
    return prev, nxt


def _rwkvfeat_body(p_ref, pv_ref, nx_ref, mu_ref, w0_ref, wup_ref, a0_ref, aup_ref, gup_ref, kk_ref, ka_ref,
                   down_ref, up_ref, jv_ref, misc_ref, ext_ref, *, tm, n_lat, seq, ctx_len):
    first, last = _seq_edges(pl.program_id(0), tm, n_lat, seq, ctx_len)
    h = V7X_SUBLANES
    ext_ref[h:h + tm, :] = p_ref[...]
    ext_ref[0:h, :] = jnp.where(first, 0.0, pv_ref[...])
    ext_ref[h + tm:h + tm + h, :] = jnp.where(last, 0.0, nx_ref[...])
    p = p_ref[...]
    prev = ext_ref[h - 1:h - 1 + tm, :]
    nxt = ext_ref[h + 1:h + 1 + tm, :]
    xs = p + mu_ref[...] * (0.5 * (prev + nxt) - p)
    W = RWKV_WIDTH
    r = xs[:, 0:W]
    k = xs[:, W:2 * W]
    v = xs[:, 2 * W:3 * W]
    o = 3 * W
    wd = xs[:, o:o + N_DIR * W_LORA]
    ad = xs[:, o + N_DIR * W_LORA:o + N_DIR * (W_LORA + A_LORA)]
    gd = xs[:, o + N_DIR * (W_LORA + A_LORA):]
    w_lin = w0_ref[...] + _dot3(jnp.tanh(wd), wup_ref[...])
    w_log = -jax.nn.softplus(-w_lin) - 0.5
    decay = jnp.exp(-jnp.exp(w_log))
    a = jax.nn.sigmoid(a0_ref[...] + _dot3(ad, aup_ref[...]))
    g = _dot3(jax.nn.sigmoid(gd), gup_ref[...])
    kk = k * kk_ref[...]
    kk = kk * lax.rsqrt(_seg_reduce(kk * kk, down_ref[...], up_ref[...]) + 1e-12)
    for d in range(N_DIR):
        a_d = a[:, d * W:(d + 1) * W]
        base = d * N_JVEC * W
        jv_ref[:, base + 0 * W:base + 1 * W] = decay[:, d * W:(d + 1) * W]
        jv_ref[:, base + 1 * W:base + 2 * W] = -kk
        jv_ref[:, base + 2 * W:base + 3 * W] = kk * a_d
        jv_ref[:, base + 3 * W:base + 4 * W] = k * (1.0 + (a_d - 1.0) * ka_ref[...])
        jv_ref[:, base + 4 * W:base + 5 * W] = r
    misc_ref[:, 0:W] = r
    misc_ref[:, W:2 * W] = k
    misc_ref[:, 2 * W:3 * W] = v
    misc_ref[:, 3 * W:4 * W] = g


def _rwkvfeat(p_rw, mu, w0, wup2, a0, aup2, gup, k_k, k_a, tm, n_lat, seq, ctx_len):
    m = p_rw.shape[0]
    down, up = _seg_matrices(RWKV_WIDTH, mean=False)
    prev, nxt = _halo_specs(tm, V7X_SUBLANES, RWKV_COLS, m)
    full = lambda a: pl.BlockSpec(a.shape, lambda i: (0,) * a.ndim)
    params = [mu, w0, wup2, a0, aup2, gup, k_k, k_a, down, up]
    return pl.pallas_call(
        functools.partial(_rwkvfeat_body, tm=tm, n_lat=n_lat, seq=seq, ctx_len=ctx_len),
        grid=(m // tm,),
        in_specs=[pl.BlockSpec((tm, RWKV_COLS), lambda i: (i, 0)), prev, nxt] + [full(a) for a in params],
        out_specs=[pl.BlockSpec((tm, N_DIR * N_JVEC * RWKV_WIDTH), lambda i: (i, 0)),
                   pl.BlockSpec((tm, 4 * RWKV_WIDTH), lambda i: (i, 0))],
        out_shape=[jax.ShapeDtypeStruct((m, N_DIR * N_JVEC * RWKV_WIDTH), F32),
                   jax.ShapeDtypeStruct((m, 4 * RWKV_WIDTH), F32)],
        scratch_shapes=[pltpu.VMEM((tm + 2 * V7X_SUBLANES, RWKV_COLS), F32)],
        compiler_params=_cparams(("parallel",)),
        name="rwkvfeat",
    )(p_rw, p_rw, p_rw, *params)


def _scan_body(*refs, ts, n_chain_grp):
    nq = n_chain_grp
    rev_ref = refs[0]
    j_refs = refs[1:1 + nq]
    v_refs = refs[1 + nq:1 + 2 * nq]
    y_refs = refs[1 + 2 * nq:1 + 2 * nq + N_DIR]
    s_ref, jt_ref, vs_ref, ys_ref, xt_ref = refs[1 + 2 * nq + N_DIR:]
    W = RWKV_WIDTH
    rep = V7X_LANES // (nq * RWKV_HEADS)

    @pl.when(pl.program_id(0) == 0)
    def _():
        s_ref[...] = jnp.zeros_like(s_ref)

    def flip(x):
        hi = x.astype(BF16)
        rest = x - hi.astype(F32)
        mid = rest.astype(BF16)
        lo = (rest - mid.astype(F32)).astype(BF16)
        rev = rev_ref[...]
        return ((jnp.dot(rev, hi, preferred_element_type=F32) + jnp.dot(rev, mid, preferred_element_type=F32))
                + jnp.dot(rev, lo, preferred_element_type=F32))

    def backward(q):
        return q >= nq // N_DIR

    for vec in range(N_JVEC):
        for q in range(nq):
            x = j_refs[q][:, vec * W:(vec + 1) * W]
            xt_ref[q] = (flip(x) if backward(q) else x).T

        def jtile(j, carry, vec=vec):
            pieces = []
            for q in range(nq):
                pieces += [xt_ref[q, pl.ds(j, RWKV_HEADS, stride=HEAD_DIM), :]] * rep
            jt_ref[vec, j] = jnp.concatenate(pieces, axis=0).T
            return carry

        lax.fori_loop(0, HEAD_DIM, jtile, 0, unroll=16)

    for q in range(nq):
        x = v_refs[q][...]
        xt_ref[q] = (flip(x) if backward(q) else x).T
    for ihi in range(SCAN_IHI):
        pieces = []
        for q in range(nq):
            for ilo in range(SCAN_ILO):
                pieces.append(xt_ref[q, pl.ds(ihi * SCAN_ILO + ilo, RWKV_HEADS, stride=HEAD_DIM), :])
        vs_ref[pl.ds(ihi, ts, stride=SCAN_IHI), :] = jnp.concatenate(pieces, axis=0).T

    n_acc = 8

    def tree_sum(parts):
        while len(parts) > 1:
            parts = [parts[k] + parts[k + 1] for k in range(0, len(parts), 2)]
        return parts[0]

    def add_part(parts, k, p):
        parts[k % n_acc] = p if parts[k % n_acc] is None else parts[k % n_acc] + p

    def step(t, sa):
        row = pl.ds(t, 1)
        nxt = pl.ds(jnp.minimum(t + 1, ts - 1), 1)
        base = pl.multiple_of(t * SCAN_IHI, SCAN_IHI)
        v = vs_ref[pl.ds(base, SCAN_IHI), :]
        yparts = [None] * n_acc
        sparts = [None] * n_acc
        for j in range(HEAD_DIM):
            sj = s_ref[j] * jt_ref[0, j, row, :] + sa * jt_ref[2, j, row, :] + v * jt_ref[3, j, row, :]
            s_ref[j] = sj
            add_part(yparts, j, sj * jt_ref[4, j, row, :])
            add_part(sparts, j, sj * jt_ref[1, j, nxt, :])
        ys_ref[pl.ds(base, SCAN_IHI), :] = tree_sum(yparts)
        return tree_sum(sparts)

    first = [None] * n_acc
    for j in range(HEAD_DIM):
        add_part(first, j, s_ref[j] * jt_ref[1, j, pl.ds(0, 1), :])
    lax.fori_loop(0, ts, step, tree_sum(first), unroll=4)

    for ihi in range(SCAN_IHI):
        yt = ys_ref[pl.ds(ihi, ts, stride=SCAN_IHI), :].T
        for q in range(nq):
            for ilo in range(SCAN_ILO):
                r0 = (q * SCAN_ILO + ilo) * RWKV_HEADS
                xt_ref[q, pl.ds(ihi * SCAN_ILO + ilo, RWKV_HEADS, stride=HEAD_DIM), :] = yt[r0:r0 + RWKV_HEADS]
    per_dir = nq // N_DIR
    for q in range(nq):
        y = xt_ref[q].T
        y_refs[q // per_dir][q % per_dir] = flip(y) if backward(q) else y


def _rwkv_scan(jv, misc, batch, seq, ctx_len, ts):
    nq = N_DIR * batch
    assert nq * RWKV_HEADS * SCAN_ILO == V7X_LANES and seq % ts == 0 and ctx_len % ts == 0
    n_l, n_c = seq // ts, ctx_len // ts
    W = RWKV_WIDTH

    def row_block(d, b):
        def f(i):
            is_ctx = i < n_c
            k_ctx = (n_c - 1 - i) if d else i
            k_lat = (n_l - 1 - (i - n_c)) if d else (i - n_c)
            return jnp.where(is_ctx, batch * n_l + b * n_c + k_ctx, b * n_l + k_lat)
        return f

    def pos_block(d):
        def f(i):
            return jnp.where(i < n_c, n_c - 1 - i, n_c + n_l - 1 - (i - n_c)) if d else i
        return f

    chains = [(d, b) for d in range(N_DIR) for b in range(batch)]
    j_specs = [pl.BlockSpec((ts, N_JVEC * W), lambda i, d=d, f=row_block(d, b): (f(i), d)) for d, b in chains]
    v_specs = [pl.BlockSpec((ts, W), lambda i, f=row_block(d, b): (f(i), 2)) for d, b in chains]
    y_specs = [pl.BlockSpec((batch, ts, W), lambda i, f=pos_block(d): (0, f(i), 0)) for d in range(N_DIR)]
    rev = jnp.asarray(np.eye(ts, dtype=np.float32)[::-1].copy(), BF16)
    return pl.pallas_call(
        functools.partial(_scan_body, ts=ts, n_chain_grp=nq),
        grid=(n_c + n_l,),
        in_specs=[pl.BlockSpec((ts, ts), lambda i: (0, 0))] + j_specs + v_specs,
        out_specs=y_specs,
        out_shape=[jax.ShapeDtypeStruct((batch, seq + ctx_len, W), F32)] * N_DIR,
        scratch_shapes=[pltpu.VMEM((HEAD_DIM, SCAN_IHI, V7X_LANES), F32),
                        pltpu.VMEM((N_JVEC, HEAD_DIM, ts, V7X_LANES), F32),
                        pltpu.VMEM((ts * SCAN_IHI, V7X_LANES), F32),
                        pltpu.VMEM((ts * SCAN_IHI, V7X_LANES), F32),
                        pltpu.VMEM((nq, W, ts), F32)],
        compiler_params=_cparams(("arbitrary",)),
        name="scan",
    )(rev, *([jv] * nq), *([misc] * nq))


def _rwkvout_body(yf_ref, yb_ref, misc_ref, rk_ref, lg_ref, lb_ref, dmean_ref, dsum_ref, up_ref, o_ref):
    W = RWKV_WIDTH
    y = yf_ref[0] + yb_ref[0]
    r = misc_ref[:, 0:W]
    k = misc_ref[:, W:2 * W]
    v = misc_ref[:, 2 * W:3 * W]
    g = misc_ref[:, 3 * W:4 * W]
    mean = _seg_reduce(y, dmean_ref[...], up_ref[...])
    yc = y - mean
    var = _seg_reduce(yc * yc, dmean_ref[...], up_ref[...])
    yn = yc * lax.rsqrt(var + GN_EPS) * lg_ref[...] + lb_ref[...]
    bonus = _seg_reduce(r * k * rk_ref[...], dsum_ref[...], up_ref[...]) * v
    o_ref[...] = (yn + bonus) * g


def _rwkvout(y2, misc, r_k, ln_g, ln_b, tm, n_lat, seq, ctx_len):
    m = misc.shape[0]

    def ypos(i):
        r0 = i * tm
        in_lat = r0 < n_lat
        b = jnp.where(in_lat, r0 // seq, (r0 - n_lat) // ctx_len)
        pos = jnp.where(in_lat, ctx_len + r0 % seq, (r0 - n_lat) % ctx_len)
        return (b, pos // tm, 0)

    dmean, up = _seg_matrices(RWKV_WIDTH, mean=True)
    dsum, _ = _seg_matrices(RWKV_WIDTH, mean=False)
    full = lambda a: pl.BlockSpec(a.shape, lambda i: (0,) * a.ndim)
    params = [r_k, ln_g, ln_b, dmean, dsum, up]
    return pl.pallas_call(
        _rwkvout_body,
        grid=(m // tm,),
        in_specs=[pl.BlockSpec((1, tm, RWKV_WIDTH), ypos), pl.BlockSpec((1, tm, RWKV_WIDTH), ypos),
                  pl.BlockSpec((tm, 4 * RWKV_WIDTH), lambda i: (i, 0))] + [full(a) for a in params],
        out_specs=pl.BlockSpec((tm, RWKV_WIDTH), lambda i: (i, 0)),
        out_shape=jax.ShapeDtypeStruct((m, RWKV_WIDTH), F32),
        compiler_params=_cparams(("parallel",)),
        name="rwkvout",
    )(y2[0], y2[1], misc, *params)


def _conv_body(p_ref, pv_ref, nx_ref, w_ref, b_ref, g_ref, o_ref, ext_ref, *, tm, n_lat, seq, ctx_len):
    first, last = _seq_edges(pl.program_id(0), tm, n_lat, seq, ctx_len)
    C = CONV_WIDTH
    H = CONV_HALO

    def glu(t):
        return t[:, :C] * jax.nn.sigmoid(t[:, C:])

    ext_ref[H:H + tm, :] = glu(p_ref[...])
    ext_ref[0:H, :] = jnp.where(first, 0.0, glu(pv_ref[...]))
    ext_ref[H + tm:H + tm + H, :] = jnp.where(last, 0.0, glu(nx_ref[...]))
    rows = 64
    for c in range(tm // rows):
        acc = jnp.zeros((rows, C), F32)
        for t in range(CONV_KSIZE):
            off = c * rows + H - CONV_PAD + t
            acc = acc + ext_ref[off:off + rows, :] * w_ref[t:t + 1, :]
        u = acc + b_ref[...]
        ms = jnp.mean(u * u, axis=-1, keepdims=True)
        z = u * lax.rsqrt(ms + EPS) * g_ref[...]
        o_ref[c * rows:(c + 1) * rows, :] = z * jax.nn.sigmoid(z)


def _conv(p_cv, w, b, g, tm, n_lat, seq, ctx_len):
    m = p_cv.shape[0]
    prev, nxt = _halo_specs(tm, CONV_HALO, 2 * CONV_WIDTH, m)
    full = lambda a: pl.BlockSpec(a.shape, lambda i: (0,) * a.ndim)
    return pl.pallas_call(
        functools.partial(_conv_body, tm=tm, n_lat=n_lat, seq=seq, ctx_len=ctx_len),
        grid=(m // tm,),
        in_specs=[pl.BlockSpec((tm, 2 * CONV_WIDTH), lambda i: (i, 0)), prev, nxt, full(w), full(b), full(g)],
        out_specs=pl.BlockSpec((tm, CONV_WIDTH), lambda i: (i, 0)),
        out_shape=jax.ShapeDtypeStruct((m, CONV_WIDTH), F32),
        scratch_shapes=[pltpu.VMEM((tm + 2 * CONV_HALO, CONV_WIDTH), F32)],
        compiler_params=_cparams(("parallel",)),
        name="conv",
    )(p_cv, p_cv, p_cv, w, b, g)


def _outproj_body(ya_ref, yr_ref, yc_ref, x_ref, g_ref, wa_ref, wr_ref, wc_ref, o_ref):
    acc = jnp.dot(ya_ref[...].astype(BF16), wa_ref[...], preferred_element_type=F32)
    acc += jnp.dot(yr_ref[...].astype(BF16), wr_ref[...], preferred_element_type=F32)
    acc += jnp.dot(yc_ref[...].astype(BF16), wc_ref[...], preferred_element_type=F32)
    o_ref[...] = x_ref[...] + g_ref[0] * acc


def _outproj(ya, yr, yc, x, gate, w, n_lat, seq, tm, tn):
    m, d = x.shape
    n_grp = gate.shape[0] - 1

    def grp(i, j):
        r0 = i * tm
        return (jnp.where(r0 < n_lat, r0 // seq, n_grp), 0, j)

    wa, wr, wc = w[:ATT_WIDTH], w[ATT_WIDTH:ATT_WIDTH + RWKV_WIDTH], w[ATT_WIDTH + RWKV_WIDTH:]
    return pl.pallas_call(
        _outproj_body,
        grid=(m // tm, d // tn),
        in_specs=[pl.BlockSpec((tm, ATT_WIDTH), lambda i, j: (i, 0)),
                  pl.BlockSpec((tm, RWKV_WIDTH), lambda i, j: (i, 0)),
                  pl.BlockSpec((tm, CONV_WIDTH), lambda i, j: (i, 0)),
                  pl.BlockSpec((tm, tn), lambda i, j: (i, j)),
                  pl.BlockSpec((1, 1, tn), grp),
                  pl.BlockSpec((ATT_WIDTH, tn), lambda i, j: (0, j)),
                  pl.BlockSpec((RWKV_WIDTH, tn), lambda i, j: (0, j)),
                  pl.BlockSpec((CONV_WIDTH, tn), lambda i, j: (0, j))],
        out_specs=pl.BlockSpec((tm, tn), lambda i, j: (i, j)),
        out_shape=jax.ShapeDtypeStruct((m, d), F32),
        compiler_params=_cparams(("parallel", "arbitrary")),
        name="outproj",
    )(ya, yr, yc, x, gate, wa, wr, wc)


def _peer_pairs():
    n = PEER_TOPK + 1
    return [(a, b) for a in range(n) for b in range(n) if (a + 1) * (b + 1) <= n]


def _top_distinct(s, n, count):
    vals, cnts = [], []
    v = s
    for _ in range(n):
        mx = jnp.max(v, axis=0, keepdims=True)
        hit = v == mx
        vals.append(mx)
        if count:
            cnt = jnp.sum(hit.astype(F32), axis=0, keepdims=True)
            cnts.append(jnp.where(mx == -jnp.inf, 0.0, cnt))
        v = jnp.where(hit, -jnp.inf, v)
    if count:
        return vals, cnts
    return vals, jnp.sum((v == -jnp.inf).astype(F32), axis=0, keepdims=True)


def _peer_head_factors(q_ref, keys_ref, h, count):
    n = PEER_TOPK + 1
    nt = (((1,), (1,)), ((), ()))
    pairs = _peer_pairs()
    sc = []
    for p in range(2):
        c0 = (h * 2 + p) * PEER_N_KEYS
        sc.append(lax.dot_general(keys_ref[h, p], q_ref[:, c0:c0 + PEER_N_KEYS], nt,
                                  preferred_element_type=F32, precision=HIGHEST))
    (v1, n1), (v2, n2) = _top_distinct(sc[0], n, count), _top_distinct(sc[1], n, count)
    cv = jnp.concatenate([v1[a] + v2[b] for a, b in pairs], axis=0)
    cm = jnp.concatenate([n1[a] * n2[b] for a, b in pairs], axis=0) if count else None
    top = v1[0] + v2[0]
    cum = jnp.zeros_like(top)
    t16 = jnp.full_like(top, -jnp.inf)
    t17 = jnp.full_like(top, -jnp.inf)
    z = jnp.zeros_like(top)
    for _ in range(n):
        mx = jnp.max(cv, axis=0, keepdims=True)
        hit = cv == mx
        hits = jnp.where(hit, cm, 0.0) if count else hit.astype(F32)
        cnt = jnp.where(mx == -jnp.inf, 0.0, jnp.sum(hits, axis=0, keepdims=True))
        new = cum + cnt
        t16 = jnp.where((cum < PEER_TOPK) & (new >= PEER_TOPK), mx, t16)
        t17 = jnp.where((cum < n) & (new >= n), mx, t17)
        take = jnp.minimum(cnt, jnp.maximum(PEER_TOPK - cum, 0.0))
        z = z + jnp.where(take > 0, take * jnp.exp(mx - top), 0.0)
        cum = new
        cv = jnp.where(hit, -jnp.inf, cv)
    thr = 0.5 * (t16 + t17)
    factors = (jnp.exp(sc[1] - v2[0]), jnp.exp(thr - sc[0] - v2[0]), jnp.exp(sc[0] - v1[0]) * (0.5 / z))
    if count:
        return factors, None
    return factors, (n1 != n).astype(F32) + (n2 != n).astype(F32)


def _peertopk_body(q_ref, keys_ref, e2_ref, a1_ref, e1_ref):
    dup = None
    for h in range(PEER_HEADS):
        (e2_ref[h], a1_ref[h], e1_ref[h]), d = _peer_head_factors(q_ref, keys_ref, h, count=False)
        dup = d if dup is None else dup + d

    @pl.when(jnp.max(dup) > 0)
    def _():
        for h in range(PEER_HEADS):
            (e2_ref[h], a1_ref[h], e1_ref[h]), _ = _peer_head_factors(q_ref, keys_ref, h, count=True)


def _peertopk(q, keys, tt):
    m, qw = q.shape
    shp = jax.ShapeDtypeStruct((PEER_HEADS, PEER_N_KEYS, m), F32)
    ospec = pl.BlockSpec((PEER_HEADS, PEER_N_KEYS, tt), lambda i: (0, 0, i))
    return pl.pallas_call(
        _peertopk_body,
        grid=(m // tt,),
        in_specs=[pl.BlockSpec((tt, qw), lambda i: (i, 0)),
                  pl.BlockSpec(keys.shape, lambda i: (0, 0, 0, 0))],
        out_specs=[ospec] * 3,
        out_shape=[shp] * 3,
        compiler_params=_cparams(("parallel",)),
        name="peertopk",
    )(q, keys)


def _peerdense_body(h_ref, u_ref, vt_ref, e2_ref, a1_ref, e1_ref, x_ref, g_ref, o_ref, acc_ref, a_ref, ht_ref,
                    st_ref, *, ec):
    e = pl.program_id(1)
    tt = ht_ref.shape[1]

    @pl.when(e == 0)
    def _():
        acc_ref[...] = jnp.zeros_like(acc_ref)
        ht_ref[...] = h_ref[...].astype(F32).T.astype(BF16)

    tc = 2 * V7X_LANES
    rh = PEER_N_KEYS // 2
    piece = 2 * PEER_N_KEYS
    for p0 in range(0, ec, piece):
        st_ref[p0:p0 + piece, :] = jnp.dot(u_ref[p0:p0 + piece, :], ht_ref[...], preferred_element_type=F32)
        for ii in range(p0 // PEER_N_KEYS, (p0 + piece) // PEER_N_KEYS):
            for c in range(tt // tc):
                cs = pl.ds(c * tc, tc)
                for r in range(PEER_N_KEYS // rh):
                    ks = pl.ds(r * rh, rh)
                    w = None
                    for h in range(PEER_HEADS):
                        e2 = e2_ref[h, ks, cs]
                        t = jnp.where(e2 >= a1_ref[h, 0, pl.ds(ii, 1), cs], e2, 0.0) * e1_ref[h, 0, pl.ds(ii, 1), cs]
                        w = t if w is None else w + t
                    rows = pl.ds(ii * PEER_N_KEYS + r * rh, rh)
                    s = st_ref[rows, cs]
                    act2 = s * (1.0 + lax.erf(s * np.float32(1.0 / np.sqrt(2.0))))
                    a_ref[rows, cs] = (act2 * w).astype(BF16)
    acc_ref[...] += jnp.dot(vt_ref[...], a_ref[...], preferred_element_type=F32)

    @pl.when(e == pl.num_programs(1) - 1)
    def _():
        o_ref[...] = x_ref[...] + g_ref[0] * acc_ref[...].T


def _peerdense(h2, u, vt, e2, a1, e1, x, gate, n_lat, seq, tt, ec, m_out):
    d = x.shape[1]
    m = m_out
    n_chunks = u.shape[0] // ec
    n_grp = gate.shape[0] - 1

    def grp(i, e):
        r0 = i * tt
        return (jnp.where(r0 < n_lat, r0 // seq, n_grp), 0, 0)

    per = ec // PEER_N_KEYS
    a1 = a1.reshape(PEER_HEADS, n_chunks, per, a1.shape[2])
    e1 = e1.reshape(PEER_HEADS, n_chunks, per, e1.shape[2])
    row_spec = pl.BlockSpec((PEER_HEADS, 1, per, tt), lambda i, e: (0, e, 0, i))
    return pl.pallas_call(
        functools.partial(_peerdense_body, ec=ec),
        grid=(m // tt, n_chunks),
        in_specs=[pl.BlockSpec((tt, d), lambda i, e: (i, 0)),
                  pl.BlockSpec((ec, d), lambda i, e: (e, 0)),
                  pl.BlockSpec((d, ec), lambda i, e: (0, e)),
                  pl.BlockSpec((PEER_HEADS, PEER_N_KEYS, tt), lambda i, e: (0, 0, i)),
                  row_spec, row_spec,
                  pl.BlockSpec((tt, d), lambda i, e: (i, 0)),
                  pl.BlockSpec((1, 1, d), grp)],
        out_specs=pl.BlockSpec((tt, d), lambda i, e: (i, 0)),
        out_shape=jax.ShapeDtypeStruct((m, d), F32),
        scratch_shapes=[pltpu.VMEM((d, tt), F32), pltpu.VMEM((ec, tt), BF16), pltpu.VMEM((d, tt), BF16),
                        pltpu.VMEM((ec, tt), F32)],
        compiler_params=_cparams(("parallel", "arbitrary")),
        name="peerdense",
    )(h2, u, vt, e2, a1, e1, x, gate)


def _tiles(seq, ctx_len, batch):
    ctx_rows = batch * ctx_len
    return dict(
        mm=min(512, seq, ctx_rows),
        seqt=min(256, seq, ctx_len),
        peer=min(512, seq, ctx_rows),
        topk=min(256, seq, ctx_rows),
        scan=min(128, seq, ctx_len),
    )


def _rope_tables(batch, seq, ctx_len):
    inv = ROPE_THETA ** (-jnp.arange(0, ROPE_AXIS_DIM, 2, dtype=F32) / ROPE_AXIS_DIM)
    t = jnp.arange(seq, dtype=jnp.int32)
    ang_r = (t // GRID_W).astype(F32)[:, None] * inv[None, :]
    ang_c = (t % GRID_W).astype(F32)[:, None] * inv[None, :]
    cos = jnp.concatenate([jnp.cos(ang_r)] * 2 + [jnp.cos(ang_c)] * 2, axis=-1)
    sin = jnp.concatenate([-jnp.sin(ang_r), jnp.sin(ang_r), -jnp.sin(ang_c), jnp.sin(ang_c)], axis=-1)
    n_ctx = batch * ctx_len
    cos = jnp.concatenate([jnp.tile(cos, (batch, 1)), jnp.ones((n_ctx, HEAD_DIM), F32)], axis=0)
    sin = jnp.concatenate([jnp.tile(sin, (batch, 1)), jnp.zeros((n_ctx, HEAD_DIM), F32)], axis=0)
    return jnp.tile(cos, (1, 2)), jnp.tile(sin, (1, 2))


def _block_diag2(w):
    z = jnp.zeros_like(w[0])
    return jnp.concatenate([jnp.concatenate([w[0], z], axis=1), jnp.concatenate([z, w[1]], axis=1)], axis=0)


def kernel(x, c, ctx, c_ctx, norm1_g, norm2_g, w_ada, b_ada, w_in, w_out, q_norm_g, k_norm_g, attn_sink, rwkv_mu, rwkv_w0, rwkv_w_up, rwkv_a0, rwkv_a_up, rwkv_g_up, rwkv_k_k, rwkv_k_a, rwkv_r_k, rwkv_ln_g, rwkv_ln_b, conv_w, conv_b, conv_norm_g, peer_wq, peer_keys, peer_u, peer_v):
    batch, seq, d = x.shape
    ctx_len = ctx.shape[1]
    depth = w_in.shape[0]
    n_lat = batch * seq
    tl = _tiles(seq, ctx_len, batch)
    assert seq % ATT_BLOCK == 0 and ctx_len % ATT_BLOCK == 0 and n_lat % ctx_len == 0
    assert (seq + ctx_len) % tl["scan"] == 0

    xt = jnp.concatenate([x.reshape(n_lat, d), ctx.reshape(batch * ctx_len, d)], axis=0)
    cc = jnp.concatenate([c, c_ctx[None, :], jnp.zeros((8 - batch - 1, d), F32)], axis=0)
    cos2, sin2 = _rope_tables(batch, seq, ctx_len)
    row = lambda a: a.reshape(1, -1)

    for i in range(depth):
        mod = _ada(cc, w_ada[i], row(b_ada[i]))[:batch + 1].reshape(batch + 1, 6, 1, d)
        sh1, sc1, g1, sh2, sc2, g2 = (mod[:, k] for k in range(6))

        w_in_b = w_in[i].astype(BF16)
        p_att, h1 = _modmm(xt, row(norm1_g[i]), sc1, sh1, w_in_b[:, :ATT_COLS], n_lat, seq, tl["mm"], ATT_COLS)
        p_rw = _mm(h1, w_in_b[:, ATT_COLS:ATT_COLS + RWKV_COLS], tl["mm"], RWKV_COLS)
        p_cv = _mm(h1, w_in_b[:, ATT_COLS + RWKV_COLS:], tl["mm"], 2 * CONV_WIDTH)

        gain = jnp.concatenate([jnp.tile(q_norm_g[i] * HEAD_DIM ** -0.5, ATT_HEADS), jnp.tile(k_norm_g[i], ATT_KV_HEADS)])
        qk, vv = _qkprep(p_att, row(gain), cos2, sin2, tl["seqt"])
        y_att = _attention(qk, vv, attn_sink[i], batch, seq, ctx_len)

        jv, misc = _rwkvfeat(p_rw, row(rwkv_mu[i]), row(rwkv_w0[i]), _block_diag2(rwkv_w_up[i]),
                             row(rwkv_a0[i]), _block_diag2(rwkv_a_up[i]), rwkv_g_up[i],
                             row(rwkv_k_k[i]), row(rwkv_k_a[i]), tl["seqt"], n_lat, seq, ctx_len)
        y2 = _rwkv_scan(jv, misc, batch, seq, ctx_len, tl["scan"])
        y_rw = _rwkvout(y2, misc, row(rwkv_r_k[i]), row(rwkv_ln_g[i]), row(rwkv_ln_b[i]), tl["seqt"],
                        n_lat, seq, ctx_len)

        y_cv = _conv(p_cv, conv_w[i], row(conv_b[i]), row(conv_norm_g[i]), tl["seqt"], n_lat, seq, ctx_len)

        xt = _outproj(y_att, y_rw, y_cv, xt, g1, w_out[i].astype(BF16), n_lat, seq, tl["mm"], d)

        q, h2 = _modmm(xt, row(norm2_g[i]), sc2, sh2, peer_wq[i].astype(BF16), n_lat, seq, tl["mm"], peer_wq.shape[2])
        e2, a1, e1 = _peertopk(q, peer_keys[i], tl["topk"])
        xt = _peerdense(h2, peer_u[i].astype(BF16), peer_v[i].T.astype(BF16), e2, a1, e1, xt, g2,
                        n_lat, seq, tl["peer"], 1024, n_lat if i == depth - 1 else xt.shape[0])
    return xt.reshape(batch, seq, d)
```

```python
import functools

import jax
import jax.numpy as jnp
import numpy as np
from jax import lax
from jax.experimental import pallas as pl
from jax.experimental.pallas import tpu as pltpu

F32 = jnp.float32
BF16 = jnp.bfloat16
HIGHEST = lax.Precision.HIGHEST

V7X_LANES = 128
V7X_SUBLANES = 8
V7X_VMEM_BYTES = 64 * 1024 * 1024
VMEM_LIMIT = V7X_VMEM_BYTES - 8 * 1024 * 1024

HEAD_DIM = 64
EPS = 1e-6
GN_EPS = 64e-5
NEG_INF = -1e30
GRID_W = 64
ATT_HEADS = 16
ATT_KV_HEADS = 4
ATT_GROUP = ATT_HEADS // ATT_KV_HEADS
ATT_WIDTH = ATT_HEADS * HEAD_DIM
KV_WIDTH = ATT_KV_HEADS * HEAD_DIM
QK_WIDTH = ATT_WIDTH + KV_WIDTH
ATT_COLS = ATT_WIDTH + 2 * KV_WIDTH
ATT_BLOCK = 128
ROPE_THETA = 10000.0
ROPE_AXIS_DIM = HEAD_DIM // 2
ROPE_HALF = ROPE_AXIS_DIM // 2
RWKV_HEADS = 8
RWKV_WIDTH = RWKV_HEADS * HEAD_DIM
W_LORA = 64
A_LORA = 64
G_LORA = 128
N_DIR = 2
RWKV_COLS = 3 * RWKV_WIDTH + N_DIR * W_LORA + N_DIR * A_LORA + G_LORA
CONV_WIDTH = 512
CONV_KSIZE = 31
CONV_PAD = (CONV_KSIZE - 1) // 2
CONV_HALO = 16
PEER_HEADS = 8
PEER_N_KEYS = 128
PEER_TOPK = 16
N_JVEC = 5
SCAN_ILO = 4
SCAN_IHI = HEAD_DIM // SCAN_ILO


def _cparams(sem):
    return pltpu.CompilerParams(dimension_semantics=sem, vmem_limit_bytes=VMEM_LIMIT)


def _seg_matrices(width, mean):
    nseg = width // HEAD_DIM
    seg = np.arange(width) // HEAD_DIM
    down = np.zeros((width, V7X_LANES), np.float32)
    down[np.arange(width), seg] = 1.0 / HEAD_DIM if mean else 1.0
    up = np.zeros((V7X_LANES, width), np.float32)
    up[seg, np.arange(width)] = 1.0
    assert nseg <= V7X_LANES
    return jnp.asarray(down, BF16), jnp.asarray(up, BF16)


def _split2(a):
    hi = a.astype(BF16)
    return hi, (a - hi.astype(F32)).astype(BF16)


def _dot_exact_rhs(a, m):
    hi, lo = _split2(a)
    return jnp.dot(hi, m, preferred_element_type=F32) + jnp.dot(lo, m, preferred_element_type=F32)


def _dot3(a, b):
    a_hi, a_lo = _split2(a)
    b_hi, b_lo = _split2(b)
    return (jnp.dot(a_hi, b_hi, preferred_element_type=F32) + jnp.dot(a_lo, b_hi, preferred_element_type=F32)
            + jnp.dot(a_hi, b_lo, preferred_element_type=F32))


def _seg_reduce(x, down, up):
    return _dot_exact_rhs(_dot_exact_rhs(x, down), up)


def _ada_body(c_ref, w_ref, b_ref, o_ref):
    cc = c_ref[...]
    s = (cc * jax.nn.sigmoid(cc)).astype(BF16)
    o_ref[...] = jnp.dot(s, w_ref[...].astype(BF16), preferred_element_type=F32) + b_ref[...]


def _ada(cc, w, b):
    d, n = w.shape
    tn = 1024
    return pl.pallas_call(
        _ada_body,
        grid=(n // tn,),
        in_specs=[pl.BlockSpec((8, d), lambda j: (0, 0)),
                  pl.BlockSpec((d, tn), lambda j: (0, j)),
                  pl.BlockSpec((1, tn), lambda j: (0, j))],
        out_specs=pl.BlockSpec((8, tn), lambda j: (0, j)),
        out_shape=jax.ShapeDtypeStruct((8, n), F32),
        compiler_params=_cparams(("arbitrary",)),
        name="ada",
    )(cc, w, b)


def _modmm_body(x_ref, g_ref, sc_ref, sh_ref, w_ref, o_ref, h_ref):
    @pl.when(pl.program_id(1) == 0)
    def _():
        x = x_ref[...]
        ms = jnp.mean(x * x, axis=-1, keepdims=True)
        y = x * lax.rsqrt(ms + EPS) * g_ref[...]
        h_ref[...] = (y * (1.0 + sc_ref[0]) + sh_ref[0]).astype(BF16)

    o_ref[...] = jnp.dot(h_ref[...], w_ref[...], preferred_element_type=F32)


def _modmm(x, g, sc, sh, w, n_lat, seq, tm, tn):
    m, d = x.shape
    n = w.shape[1]
    n_grp = sc.shape[0] - 1

    def grp(i, j):
        r0 = i * tm
        return (jnp.where(r0 < n_lat, r0 // seq, n_grp), 0, 0)

    return pl.pallas_call(
        _modmm_body,
        grid=(m // tm, n // tn),
        in_specs=[pl.BlockSpec((tm, d), lambda i, j: (i, 0)),
                  pl.BlockSpec((1, d), lambda i, j: (0, 0)),
                  pl.BlockSpec((1, 1, d), grp),
                  pl.BlockSpec((1, 1, d), grp),
                  pl.BlockSpec((d, tn), lambda i, j: (0, j))],
        out_specs=[pl.BlockSpec((tm, tn), lambda i, j: (i, j)),
                   pl.BlockSpec((tm, d), lambda i, j: (i, 0))],
        out_shape=[jax.ShapeDtypeStruct((m, n), F32), jax.ShapeDtypeStruct((m, d), BF16)],
        compiler_params=_cparams(("parallel", "arbitrary")),
        name="modmm",
    )(x, g, sc, sh, w)


def _mm_body(h_ref, w_ref, o_ref):
    o_ref[...] = jnp.dot(h_ref[...], w_ref[...], preferred_element_type=F32)


def _mm(h, w, tm, tn):
    m, d = h.shape
    n = w.shape[1]
    return pl.pallas_call(
        _mm_body,
        grid=(m // tm, n // tn),
        in_specs=[pl.BlockSpec((tm, d), lambda i, j: (i, 0)),
                  pl.BlockSpec((d, tn), lambda i, j: (0, j))],
        out_specs=pl.BlockSpec((tm, tn), lambda i, j: (i, j)),
        out_shape=jax.ShapeDtypeStruct((m, n), F32),
        compiler_params=_cparams(("parallel", "arbitrary")),
        name="mm",
    )(h, w)


def _qkprep_body(p_ref, gain_ref, cos_ref, sin_ref, down_ref, up_ref, qk_ref, v_ref):
    x = p_ref[:, :QK_WIDTH]
    ms = _seg_reduce(x * x, down_ref[...], up_ref[...])
    y = x * lax.rsqrt(ms + EPS) * gain_ref[...]
    cos = cos_ref[...]
    sin = sin_ref[...]
    lane = lax.broadcasted_iota(jnp.int32, cos.shape, 1)
    first = (lane % ROPE_AXIS_DIM) < ROPE_HALF
    for blk in range(QK_WIDTH // V7X_LANES):
        yb = y[:, blk * V7X_LANES:(blk + 1) * V7X_LANES]
        partner = jnp.where(first, pltpu.roll(yb, V7X_LANES - ROPE_HALF, 1), pltpu.roll(yb, ROPE_HALF, 1))
        qk_ref[:, blk * V7X_LANES:(blk + 1) * V7X_LANES] = (yb * cos + partner * sin).astype(BF16)
    v_ref[...] = p_ref[:, QK_WIDTH:].astype(BF16)


def _qkprep(p_att, gain, cos2, sin2, tm):
    m = p_att.shape[0]
    down, up = _seg_matrices(QK_WIDTH, mean=True)
    return pl.pallas_call(
        _qkprep_body,
        grid=(m // tm,),
        in_specs=[pl.BlockSpec((tm, ATT_COLS), lambda i: (i, 0)),
                  pl.BlockSpec((1, QK_WIDTH), lambda i: (0, 0)),
                  pl.BlockSpec((tm, V7X_LANES), lambda i: (i, 0)),
                  pl.BlockSpec((tm, V7X_LANES), lambda i: (i, 0)),
                  pl.BlockSpec(down.shape, lambda i: (0, 0)),
                  pl.BlockSpec(up.shape, lambda i: (0, 0))],
        out_specs=[pl.BlockSpec((tm, QK_WIDTH), lambda i: (i, 0)),
                   pl.BlockSpec((tm, KV_WIDTH), lambda i: (i, 0))],
        out_shape=[jax.ShapeDtypeStruct((m, QK_WIDTH), BF16), jax.ShapeDtypeStruct((m, KV_WIDTH), BF16)],
        compiler_params=_cparams(("parallel",)),
        name="qkprep",
    )(p_att, gain, cos2, sin2, down, up)


def _attn_body(sink_ref, q_ref, k0_ref, k1_ref, k2_ref, v0_ref, v1_ref, v2_ref, kc_ref, vc_ref, o_ref, *, nb):
    n = pl.program_id(1)
    is_lat = n < nb
    rows = ATT_GROUP * ATT_BLOCK
    qi = lax.broadcasted_iota(jnp.int32, (rows, 3 * ATT_BLOCK), 0) % ATT_BLOCK
    si = lax.broadcasted_iota(jnp.int32, (rows, 3 * ATT_BLOCK), 1)
    kblk = n - 1 + si // ATT_BLOCK
    mask = (jnp.abs(si - ATT_BLOCK - qi) <= ATT_BLOCK) & (kblk >= 0) & (kblk < nb) & is_lat
    nt = (((1,), (1,)), ((), ()))
    for g in range(ATT_KV_HEADS):
        ks = slice(g * HEAD_DIM, (g + 1) * HEAD_DIM)
        qs = jnp.concatenate(
            [q_ref[:, (g * ATT_GROUP + a) * HEAD_DIM:(g * ATT_GROUP + a + 1) * HEAD_DIM] for a in range(ATT_GROUP)],
            axis=0)
        kw = jnp.concatenate([k0_ref[:, ks], k1_ref[:, ks], k2_ref[:, ks]], axis=0)
        vw = jnp.concatenate([v0_ref[:, ks], v1_ref[:, ks], v2_ref[:, ks]], axis=0)
        s_win = jnp.where(mask, lax.dot_general(qs, kw, nt, preferred_element_type=F32), NEG_INF)
        s_ctx = lax.dot_general(qs, kc_ref[:, ks], nt, preferred_element_type=F32)
        row_head = lax.broadcasted_iota(jnp.int32, (rows, 1), 0) // ATT_BLOCK
        snk = jnp.zeros((rows, 1), F32)
        for a in range(ATT_GROUP):
            snk = jnp.where(row_head == a, sink_ref[g * ATT_GROUP + a], snk)
        mx = jnp.maximum(jnp.maximum(jnp.max(s_win, axis=-1, keepdims=True),
                                     jnp.max(s_ctx, axis=-1, keepdims=True)), snk)
        p_win = jnp.exp(s_win - mx)
        p_ctx = jnp.exp(s_ctx - mx)
        den = (jnp.sum(p_win, axis=-1, keepdims=True) + jnp.sum(p_ctx, axis=-1, keepdims=True)
               + jnp.exp(snk - mx))
        acc = (jnp.dot(p_win.astype(BF16), vw, preferred_element_type=F32)
               + jnp.dot(p_ctx.astype(BF16), vc_ref[:, ks], preferred_element_type=F32))
        out = acc / den
        for a in range(ATT_GROUP):
            h = g * ATT_GROUP + a
            o_ref[:, h * HEAD_DIM:(h + 1) * HEAD_DIM] = out[a * ATT_BLOCK:(a + 1) * ATT_BLOCK]


def _attention(qk, v, sink, batch, seq, ctx_len):
    m = qk.shape[0]
    nb = seq // ATT_BLOCK
    ncb = ctx_len // ATT_BLOCK
    ctx_blk0 = batch * nb
    kcol = ATT_WIDTH // KV_WIDTH

    def qrow(b, n):
        return jnp.where(n < nb, b * nb + n, ctx_blk0 + b * ncb + (n - nb))

    def krow(off):
        def f(b, n, *_):
            return (b * nb + jnp.clip(n + off, 0, nb - 1), kcol)
        return f

    def vrow(off):
        def f(b, n, *_):
            return (b * nb + jnp.clip(n + off, 0, nb - 1), 0)
        return f

    ctx_row = (batch * seq) // ctx_len
    grid_spec = pltpu.PrefetchScalarGridSpec(
        num_scalar_prefetch=1,
        grid=(batch, nb + ncb),
        in_specs=[pl.BlockSpec((ATT_BLOCK, ATT_WIDTH), lambda b, n, *_: (qrow(b, n), 0)),
                  pl.BlockSpec((ATT_BLOCK, KV_WIDTH), krow(-1)),
                  pl.BlockSpec((ATT_BLOCK, KV_WIDTH), krow(0)),
                  pl.BlockSpec((ATT_BLOCK, KV_WIDTH), krow(1)),
                  pl.BlockSpec((ATT_BLOCK, KV_WIDTH), vrow(-1)),
                  pl.BlockSpec((ATT_BLOCK, KV_WIDTH), vrow(0)),
                  pl.BlockSpec((ATT_BLOCK, KV_WIDTH), vrow(1)),
                  pl.BlockSpec((ctx_len, KV_WIDTH), lambda b, n, *_: (ctx_row + b, kcol)),
                  pl.BlockSpec((ctx_len, KV_WIDTH), lambda b, n, *_: (ctx_row + b, 0))],
        out_specs=pl.BlockSpec((ATT_BLOCK, ATT_WIDTH), lambda b, n, *_: (qrow(b, n), 0)),
    )
    return pl.pallas_call(
        functools.partial(_attn_body, nb=nb),
        grid_spec=grid_spec,
        out_shape=jax.ShapeDtypeStruct((m, ATT_WIDTH), F32),
        compiler_params=_cparams(("parallel", "arbitrary")),
        name="attention",
    )(sink, qk, qk, qk, qk, v, v, v, qk, v)


def _seq_edges(i, tm, n_lat, seq, ctx_len):
    r0 = i * tm
    in_lat = r0 < n_lat
    pos = jnp.where(in_lat, r0 % seq, (r0 - n_lat) % ctx_len)
    length = jnp.where(in_lat, seq, ctx_len)
    return pos == 0, pos + tm == length


def _halo_specs(tm, halo, width, n_rows):
    per = tm // halo
    last = n_rows // halo - 1
    prev = pl.BlockSpec((halo, width), lambda i: (jnp.maximum(i * per - 1, 0), 0))
    nxt = pl.BlockSpec((halo, width), lambda i: (jnp.minimum((i + 1) * per, last), 0))
    return prev, nxt


def _rwkvfeat_body(p_ref, pv_ref, nx_ref, mu_ref, w0_ref, wup_ref, a0_ref, aup_ref, gup_ref, kk_ref, ka_ref,
                   down_ref, up_ref, jv_ref, misc_ref, ext_ref, *, tm, n_lat, seq, ctx_len):
    first, last = _seq_edges(pl.program_id(0), tm, n_lat, seq, ctx_len)
    h = V7X_SUBLANES
    ext_ref[h:h + tm, :] = p_ref[...]
    ext_ref[0:h, :] = jnp.where(first, 0.0, pv_ref[...])
    ext_ref[h + tm:h + tm + h, :] = jnp.where(last, 0.0, nx_ref[...])
    p = p_ref[...]
    prev = ext_ref[h - 1:h - 1 + tm, :]
    nxt = ext_ref[h + 1:h + 1 + tm, :]
    xs = p + mu_ref[...] * (0.5 * (prev + nxt) - p)
    W = RWKV_WIDTH
    r = xs[:, 0:W]
    k = xs[:, W:2 * W]
    v = xs[:, 2 * W:3 * W]
    o = 3 * W
    wd = xs[:, o:o + N_DIR * W_LORA]
    ad = xs[:, o + N_DIR * W_LORA:o + N_DIR * (W_LORA + A_LORA)]
    gd = xs[:, o + N_DIR * (W_LORA + A_LORA):]
    w_lin = w0_ref[...] + _dot3(jnp.tanh(wd), wup_ref[...])
    w_log = -jax.nn.softplus(-w_lin) - 0.5
    decay = jnp.exp(-jnp.exp(w_log))
    a = jax.nn.sigmoid(a0_ref[...] + _dot3(ad, aup_ref[...]))
    g = _dot3(jax.nn.sigmoid(gd), gup_ref[...])
    kk = k * kk_ref[...]
    kk = kk * lax.rsqrt(_seg_reduce(kk * kk, down_ref[...], up_ref[...]) + 1e-12)
    for d in range(N_DIR):
        a_d = a[:, d * W:(d + 1) * W]
        base = d * N_JVEC * W
        jv_ref[:, base + 0 * W:base + 1 * W] = decay[:, d * W:(d + 1) * W]
        jv_ref[:, base + 1 * W:base + 2 * W] = -kk
        jv_ref[:, base + 2 * W:base + 3 * W] = kk * a_d
        jv_ref[:, base + 3 * W:base + 4 * W] = k * (1.0 + (a_d - 1.0) * ka_ref[...])
        jv_ref[:, base + 4 * W:base + 5 * W] = r
    misc_ref[:, 0:W] = r
    misc_ref[:, W:2 * W] = k
    misc_ref[:, 2 * W:3 * W] = v
    misc_ref[:, 3 * W:4 * W] = g


def _rwkvfeat(p_rw, mu, w0, wup2, a0, aup2, gup, k_k, k_a, tm, n_lat, seq, ctx_len):
    m = p_rw.shape[0]
    down, up = _seg_matrices(RWKV_WIDTH, mean=False)
    prev, nxt = _halo_specs(tm, V7X_SUBLANES, RWKV_COLS, m)
    full = lambda a: pl.BlockSpec(a.shape, lambda i: (0,) * a.ndim)
    params = [mu, w0, wup2, a0, aup2, gup, k_k, k_a, down, up]
    return pl.pallas_call(
        functools.partial(_rwkvfeat_body, tm=tm, n_lat=n_lat, seq=seq, ctx_len=ctx_len),
        grid=(m // tm,),
        in_specs=[pl.BlockSpec((tm, RWKV_COLS), lambda i: (i, 0)), prev, nxt] + [full(a) for a in params],
        out_specs=[pl.BlockSpec((tm, N_DIR * N_JVEC * RWKV_WIDTH), lambda i: (i, 0)),
                   pl.BlockSpec((tm, 4 * RWKV_WIDTH), lambda i: (i, 0))],
        out_shape=[jax.ShapeDtypeStruct((m, N_DIR * N_JVEC * RWKV_WIDTH), F32),
                   jax.ShapeDtypeStruct((m, 4 * RWKV_WIDTH), F32)],
        scratch_shapes=[pltpu.VMEM((tm + 2 * V7X_SUBLANES, RWKV_COLS), F32)],
        compiler_params=_cparams(("parallel",)),
        name="rwkvfeat",
    )(p_rw, p_rw, p_rw, *params)


def _scan_body(*refs, ts, n_chain_grp):
    nq = n_chain_grp
    rev_ref = refs[0]
    j_refs = refs[1:1 + nq]
    v_refs = refs[1 + nq:1 + 2 * nq]
    y_refs = refs[1 + 2 * nq:1 + 2 * nq + N_DIR]
    s_ref, jt_ref, vs_ref, ys_ref, xt_ref = refs[1 + 2 * nq + N_DIR:]
    W = RWKV_WIDTH
    rep = V7X_LANES // (nq * RWKV_HEADS)

    @pl.when(pl.program_id(0) == 0)
    def _():
        s_ref[...] = jnp.zeros_like(s_ref)

    def flip(x):
        hi = x.astype(BF16)
        rest = x - hi.astype(F32)
        mid = rest.astype(BF16)
        lo = (rest - mid.astype(F32)).astype(BF16)
        rev = rev_ref[...]
        return ((jnp.dot(rev, hi, preferred_element_type=F32) + jnp.dot(rev, mid, preferred_element_type=F32))
                + jnp.dot(rev, lo, preferred_element_type=F32))

    def backward(q):
        return q >= nq // N_DIR

    for vec in range(N_JVEC):
        for q in range(nq):
            x = j_refs[q][:, vec * W:(vec + 1) * W]
            xt_ref[q] = (flip(x) if backward(q) else x).T

        def jtile(j, carry, vec=vec):
            pieces = []
            for q in range(nq):
                pieces += [xt_ref[q, pl.ds(j, RWKV_HEADS, stride=HEAD_DIM), :]] * rep
            jt_ref[vec, j] = jnp.concatenate(pieces, axis=0).T
            return carry

        lax.fori_loop(0, HEAD_DIM, jtile, 0, unroll=16)

    for q in range(nq):
        x = v_refs[q][...]
        xt_ref[q] = (flip(x) if backward(q) else x).T
    for ihi in range(SCAN_IHI):
        pieces = []
        for q in range(nq):
            for ilo in range(SCAN_ILO):
                pieces.append(xt_ref[q, pl.ds(ihi * SCAN_ILO + ilo, RWKV_HEADS, stride=HEAD_DIM), :])
        vs_ref[pl.ds(ihi, ts, stride=SCAN_IHI), :] = jnp.concatenate(pieces, axis=0).T

    n_acc = 8

    def tree_sum(parts):
        while len(parts) > 1:
            parts = [parts[k] + parts[k + 1] for k in range(0, len(parts), 2)]
        return parts[0]

    def add_part(parts, k, p):
        parts[k % n_acc] = p if parts[k % n_acc] is None else parts[k % n_acc] + p

    def step(t, sa):
        row = pl.ds(t, 1)
        nxt = pl.ds(jnp.minimum(t + 1, ts - 1), 1)
        base = pl.multiple_of(t * SCAN_IHI, SCAN_IHI)
        v = vs_ref[pl.ds(base, SCAN_IHI), :]
        yparts = [None] * n_acc
        sparts = [None] * n_acc
        for j in range(HEAD_DIM):
            sj = s_ref[j] * jt_ref[0, j, row, :] + sa * jt_ref[2, j, row, :] + v * jt_ref[3, j, row, :]
            s_ref[j] = sj
            add_part(yparts, j, sj * jt_ref[4, j, row, :])
            add_part(sparts, j, sj * jt_ref[1, j, nxt, :])
        ys_ref[pl.ds(base, SCAN_IHI), :] = tree_sum(yparts)
        return tree_sum(sparts)

    first = [None] * n_acc
    for j in range(HEAD_DIM):
        add_part(first, j, s_ref[j] * jt_ref[1, j, pl.ds(0, 1), :])
    lax.fori_loop(0, ts, step, tree_sum(first), unroll=8)

    for ihi in range(SCAN_IHI):
        yt = ys_ref[pl.ds(ihi, ts, stride=SCAN_IHI), :].T
        for q in range(nq):
            for ilo in range(SCAN_ILO):
                r0 = (q * SCAN_ILO + ilo) * RWKV_HEADS
                xt_ref[q, pl.ds(ihi * SCAN_ILO + ilo, RWKV_HEADS, stride=HEAD_DIM), :] = yt[r0:r0 + RWKV_HEADS]
    per_dir = nq // N_DIR
    for q in range(nq):
        y = xt_ref[q].T
        y_refs[q // per_dir][q % per_dir] = flip(y) if backward(q) else y


def _rwkv_scan(jv, misc, batch, seq, ctx_len, ts):
    nq = N_DIR * batch
    assert nq * RWKV_HEADS * SCAN_ILO == V7X_LANES and seq % ts == 0 and ctx_len % ts == 0
    n_l, n_c = seq // ts, ctx_len // ts
    W = RWKV_WIDTH

    def row_block(d, b):
        def f(i):
            is_ctx = i < n_c
            k_ctx = (n_c - 1 - i) if d else i
            k_lat = (n_l - 1 - (i - n_c)) if d else (i - n_c)
            return jnp.where(is_ctx, batch * n_l + b * n_c + k_ctx, b * n_l + k_lat)
        return f

    def pos_block(d):
        def f(i):
            return jnp.where(i < n_c, n_c - 1 - i, n_c + n_l - 1 - (i - n_c)) if d else i
        return f

    chains = [(d, b) for d in range(N_DIR) for b in range(batch)]
    j_specs = [pl.BlockSpec((ts, N_JVEC * W), lambda i, d=d, f=row_block(d, b): (f(i), d)) for d, b in chains]
    v_specs = [pl.BlockSpec((ts, W), lambda i, f=row_block(d, b): (f(i), 2)) for d, b in chains]
    y_specs = [pl.BlockSpec((batch, ts, W), lambda i, f=pos_block(d): (0, f(i), 0)) for d in range(N_DIR)]
    rev = jnp.asarray(np.eye(ts, dtype=np.float32)[::-1].copy(), BF16)
    return pl.pallas_call(
        functools.partial(_scan_body, ts=ts, n_chain_grp=nq),
        grid=(n_c + n_l,),
        in_specs=[pl.BlockSpec((ts, ts), lambda i: (0, 0))] + j_specs + v_specs,
        out_specs=y_specs,
        out_shape=[jax.ShapeDtypeStruct((batch, seq + ctx_len, W), F32)] * N_DIR,
        scratch_shapes=[pltpu.VMEM((HEAD_DIM, SCAN_IHI, V7X_LANES), F32),
                        pltpu.VMEM((N_JVEC, HEAD_DIM, ts, V7X_LANES), F32),
                        pltpu.VMEM((ts * SCAN_IHI, V7X_LANES), F32),
                        pltpu.VMEM((ts * SCAN_IHI, V7X_LANES), F32),
                        pltpu.VMEM((nq, W, ts), F32)],
        compiler_params=_cparams(("arbitrary",)),
        name="scan",
    )(rev, *([jv] * nq), *([misc] * nq))


def _rwkvout_body(yf_ref, yb_ref, misc_ref, rk_ref, lg_ref, lb_ref, dmean_ref, dsum_ref, up_ref, o_ref):
    W = RWKV_WIDTH
    y = yf_ref[0] + yb_ref[0]
    r = misc_ref[:, 0:W]
    k = misc_ref[:, W:2 * W]
    v = misc_ref[:, 2 * W:3 * W]
    g = misc_ref[:, 3 * W:4 * W]
    mean = _seg_reduce(y, dmean_ref[...], up_ref[...])
    yc = y - mean
    var = _seg_reduce(yc * yc, dmean_ref[...], up_ref[...])
    yn = yc * lax.rsqrt(var + GN_EPS) * lg_ref[...] + lb_ref[...]
    bonus = _seg_reduce(r * k * rk_ref[...], dsum_ref[...], up_ref[...]) * v
    o_ref[...] = (yn + bonus) * g


def _rwkvout(y2, misc, r_k, ln_g, ln_b, tm, n_lat, seq, ctx_len):
    m = misc.shape[0]

    def ypos(i):
        r0 = i * tm
        in_lat = r0 < n_lat
        b = jnp.where(in_lat, r0 // seq, (r0 - n_lat) // ctx_len)
        pos = jnp.where(in_lat, ctx_len + r0 % seq, (r0 - n_lat) % ctx_len)
        return (b, pos // tm, 0)

    dmean, up = _seg_matrices(RWKV_WIDTH, mean=True)
    dsum, _ = _seg_matrices(RWKV_WIDTH, mean=False)
    full = lambda a: pl.BlockSpec(a.shape, lambda i: (0,) * a.ndim)
    params = [r_k, ln_g, ln_b, dmean, dsum, up]
    return pl.pallas_call(
        _rwkvout_body,
        grid=(m // tm,),
        in_specs=[pl.BlockSpec((1, tm, RWKV_WIDTH), ypos), pl.BlockSpec((1, tm, RWKV_WIDTH), ypos),
                  pl.BlockSpec((tm, 4 * RWKV_WIDTH), lambda i: (i, 0))] + [full(a) for a in params],
        out_specs=pl.BlockSpec((tm, RWKV_WIDTH), lambda i: (i, 0)),
        out_shape=jax.ShapeDtypeStruct((m, RWKV_WIDTH), F32),
        compiler_params=_cparams(("parallel",)),
        name="rwkvout",
    )(y2[0], y2[1], misc, *params)


def _conv_body(p_ref, pv_ref, nx_ref, w_ref, b_ref, g_ref, o_ref, ext_ref, *, tm, n_lat, seq, ctx_len):
    first, last = _seq_edges(pl.program_id(0), tm, n_lat, seq, ctx_len)
    C = CONV_WIDTH
    H = CONV_HALO

    def glu(t):
        return t[:, :C] * jax.nn.sigmoid(t[:, C:])

    ext_ref[H:H + tm, :] = glu(p_ref[...])
    ext_ref[0:H, :] = jnp.where(first, 0.0, glu(pv_ref[...]))
    ext_ref[H + tm:H + tm + H, :] = jnp.where(last, 0.0, glu(nx_ref[...]))
    rows = 64
    for c in range(tm // rows):
        acc = jnp.zeros((rows, C), F32)
        for t in range(CONV_KSIZE):
            off = c * rows + H - CONV_PAD + t
            acc = acc + ext_ref[off:off + rows, :] * w_ref[t:t + 1, :]
        u = acc + b_ref[...]
        ms = jnp.mean(u * u, axis=-1, keepdims=True)
        z = u * lax.rsqrt(ms + EPS) * g_ref[...]
        o_ref[c * rows:(c + 1) * rows, :] = z * jax.nn.sigmoid(z)


def _conv(p_cv, w, b, g, tm, n_lat, seq, ctx_len):
    m = p_cv.shape[0]
    prev, nxt = _halo_specs(tm, CONV_HALO, 2 * CONV_WIDTH, m)
    full = lambda a: pl.BlockSpec(a.shape, lambda i: (0,) * a.ndim)
    return pl.pallas_call(
        functools.partial(_conv_body, tm=tm, n_lat=n_lat, seq=seq, ctx_len=ctx_len),
        grid=(m // tm,),
        in_specs=[pl.BlockSpec((tm, 2 * CONV_WIDTH), lambda i: (i, 0)), prev, nxt, full(w), full(b), full(g)],
        out_specs=pl.BlockSpec((tm, CONV_WIDTH), lambda i: (i, 0)),
        out_shape=jax.ShapeDtypeStruct((m, CONV_WIDTH), F32),
        scratch_shapes=[pltpu.VMEM((tm + 2 * CONV_HALO, CONV_WIDTH), F32)],
        compiler_params=_cparams(("parallel",)),
        name="conv",
    )(p_cv, p_cv, p_cv, w, b, g)


def _outproj_body(ya_ref, yr_ref, yc_ref, x_ref, g_ref, wa_ref, wr_ref, wc_ref, o_ref):
    acc = jnp.dot(ya_ref[...].astype(BF16), wa_ref[...], preferred_element_type=F32)
    acc += jnp.dot(yr_ref[...].astype(BF16), wr_ref[...], preferred_element_type=F32)
    acc += jnp.dot(yc_ref[...].astype(BF16), wc_ref[...], preferred_element_type=F32)
    o_ref[...] = x_ref[...] + g_ref[0] * acc


def _outproj(ya, yr, yc, x, gate, w, n_lat, seq, tm, tn):
    m, d = x.shape
    n_grp = gate.shape[0] - 1

    def grp(i, j):
        r0 = i * tm
        return (jnp.where(r0 < n_lat, r0 // seq, n_grp), 0, j)

    wa, wr, wc = w[:ATT_WIDTH], w[ATT_WIDTH:ATT_WIDTH + RWKV_WIDTH], w[ATT_WIDTH + RWKV_WIDTH:]
    return pl.pallas_call(
        _outproj_body,
        grid=(m // tm, d // tn),
        in_specs=[pl.BlockSpec((tm, ATT_WIDTH), lambda i, j: (i, 0)),
                  pl.BlockSpec((tm, RWKV_WIDTH), lambda i, j: (i, 0)),
                  pl.BlockSpec((tm, CONV_WIDTH), lambda i, j: (i, 0)),
                  pl.BlockSpec((tm, tn), lambda i, j: (i, j)),
                  pl.BlockSpec((1, 1, tn), grp),
                  pl.BlockSpec((ATT_WIDTH, tn), lambda i, j: (0, j)),
                  pl.BlockSpec((RWKV_WIDTH, tn), lambda i, j: (0, j)),
                  pl.BlockSpec((CONV_WIDTH, tn), lambda i, j: (0, j))],
        out_specs=pl.BlockSpec((tm, tn), lambda i, j: (i, j)),
        out_shape=jax.ShapeDtypeStruct((m, d), F32),
        compiler_params=_cparams(("parallel", "arbitrary")),
        name="outproj",
    )(ya, yr, yc, x, gate, wa, wr, wc)


def _peer_pairs():
    n = PEER_TOPK + 1
    return [(a, b) for a in range(n) for b in range(n) if (a + 1) * (b + 1) <= n]


def _top_distinct(s, n, count):
    vals, cnts = [], []
    v = s
    for _ in range(n):
        mx = jnp.max(v, axis=0, keepdims=True)
        hit = v == mx
        vals.append(mx)
        if count:
            cnt = jnp.sum(hit.astype(F32), axis=0, keepdims=True)
            cnts.append(jnp.where(mx == -jnp.inf, 0.0, cnt))
        v = jnp.where(hit, -jnp.inf, v)
    if count:
        return vals, cnts
    return vals, jnp.sum((v == -jnp.inf).astype(F32), axis=0, keepdims=True)


def _peer_head_factors(q_ref, keys_ref, h, count):
    n = PEER_TOPK + 1
    nt = (((1,), (1,)), ((), ()))
    pairs = _peer_pairs()
    sc = []
    for p in range(2):
        c0 = (h * 2 + p) * PEER_N_KEYS
        sc.append(lax.dot_general(keys_ref[h, p], q_ref[:, c0:c0 + PEER_N_KEYS], nt,
                                  preferred_element_type=F32, precision=HIGHEST))
    (v1, n1), (v2, n2) = _top_distinct(sc[0], n, count), _top_distinct(sc[1], n, count)
    cv = jnp.concatenate([v1[a] + v2[b] for a, b in pairs], axis=0)
    cm = jnp.concatenate([n1[a] * n2[b] for a, b in pairs], axis=0) if count else None
    top = v1[0] + v2[0]
    cum = jnp.zeros_like(top)
    t16 = jnp.full_like(top, -jnp.inf)
    t17 = jnp.full_like(top, -jnp.inf)
    z = jnp.zeros_like(top)
    for _ in range(n):
        mx = jnp.max(cv, axis=0, keepdims=True)
        hit = cv == mx
        hits = jnp.where(hit, cm, 0.0) if count else hit.astype(F32)
        cnt = jnp.where(mx == -jnp.inf, 0.0, jnp.sum(hits, axis=0, keepdims=True))
        new = cum + cnt
        t16 = jnp.where((cum < PEER_TOPK) & (new >= PEER_TOPK), mx, t16)
        t17 = jnp.where((cum < n) & (new >= n), mx, t17)
        take = jnp.minimum(cnt, jnp.maximum(PEER_TOPK - cum, 0.0))
        z = z + jnp.where(take > 0, take * jnp.exp(mx - top), 0.0)
        cum = new
        cv = jnp.where(hit, -jnp.inf, cv)
    thr = 0.5 * (t16 + t17)
    factors = (jnp.exp(sc[1] - v2[0]), jnp.exp(thr - sc[0] - v2[0]), jnp.exp(sc[0] - v1[0]) * (0.5 / z))
    if count:
        return factors, None
    return factors, (n1 != n).astype(F32) + (n2 != n).astype(F32)


def _peertopk_body(q_ref, keys_ref, e2_ref, a1_ref, e1_ref):
    dup = None
    for h in range(PEER_HEADS):
        (e2_ref[h], a1_ref[h], e1_ref[h]), d = _peer_head_factors(q_ref, keys_ref, h, count=False)
        dup = d if dup is None else dup + d

    @pl.when(jnp.max(dup) > 0)
    def _():
        for h in range(PEER_HEADS):
            (e2_ref[h], a1_ref[h], e1_ref[h]), _ = _peer_head_factors(q_ref, keys_ref, h, count=True)


def _peertopk(q, keys, tt):
    m, qw = q.shape
    shp = jax.ShapeDtypeStruct((PEER_HEADS, PEER_N_KEYS, m), F32)
    ospec = pl.BlockSpec((PEER_HEADS, PEER_N_KEYS, tt), lambda i: (0, 0, i))
    return pl.pallas_call(
        _peertopk_body,
        grid=(m // tt,),
        in_specs=[pl.BlockSpec((tt, qw), lambda i: (i, 0)),
                  pl.BlockSpec(keys.shape, lambda i: (0, 0, 0, 0))],
        out_specs=[ospec] * 3,
        out_shape=[shp] * 3,
        compiler_params=_cparams(("parallel",)),
        name="peertopk",
    )(q, keys)


def _peerdense_body(h_ref, u_ref, vt_ref, e2_ref, a1_ref, e1_ref, x_ref, g_ref, o_ref, acc_ref, a_ref, ht_ref,
                    st_ref, *, ec):
    e = pl.program_id(1)
    tt = ht_ref.shape[1]

    @pl.when(e == 0)
    def _():
        acc_ref[...] = jnp.zeros_like(acc_ref)
        ht_ref[...] = h_ref[...].astype(F32).T.astype(BF16)

    tc = 2 * V7X_LANES
    rh = PEER_N_KEYS // 2
    piece = 2 * PEER_N_KEYS
    for p0 in range(0, ec, piece):
        st_ref[p0:p0 + piece, :] = jnp.dot(u_ref[p0:p0 + piece, :], ht_ref[...], preferred_element_type=F32)
        if p0 == ec - piece:
            acc_ref[...] += jnp.dot(vt_ref[:, :p0], a_ref[:p0, :], preferred_element_type=F32)
        for ii in range(p0 // PEER_N_KEYS, (p0 + piece) // PEER_N_KEYS):
            for c in range(tt // tc):
                cs = pl.ds(c * tc, tc)
                for r in range(PEER_N_KEYS // rh):
                    ks = pl.ds(r * rh, rh)
                    w = None
                    for h in range(PEER_HEADS):
                        e2 = e2_ref[h, ks, cs]
                        t = jnp.where(e2 >= a1_ref[h, 0, pl.ds(ii, 1), cs], e2, 0.0) * e1_ref[h, 0, pl.ds(ii, 1), cs]
                        w = t if w is None else w + t
                    rows = pl.ds(ii * PEER_N_KEYS + r * rh, rh)
                    s = st_ref[rows, cs]
                    act2 = s * (1.0 + lax.erf(s * np.float32(1.0 / np.sqrt(2.0))))
                    a_ref[rows, cs] = (act2 * w).astype(BF16)
    acc_ref[...] += jnp.dot(vt_ref[:, ec - piece:], a_ref[ec - piece:, :], preferred_element_type=F32)

    @pl.when(e == pl.num_programs(1) - 1)
    def _():
        o_ref[...] = x_ref[...] + g_ref[0] * acc_ref[...].T


def _peerdense(h2, u, vt, e2, a1, e1, x, gate, n_lat, seq, tt, ec, m_out):
    d = x.shape[1]
    m = m_out
    n_chunks = u.shape[0] // ec
    n_grp = gate.shape[0] - 1

    def grp(i, e):
        r0 = i * tt
        return (jnp.where(r0 < n_lat, r0 // seq, n_grp), 0, 0)

    per = ec // PEER_N_KEYS
    a1 = a1.reshape(PEER_HEADS, n_chunks, per, a1.shape[2])
    e1 = e1.reshape(PEER_HEADS, n_chunks, per, e1.shape[2])
    row_spec = pl.BlockSpec((PEER_HEADS, 1, per, tt), lambda i, e: (0, e, 0, i))
    return pl.pallas_call(
        functools.partial(_peerdense_body, ec=ec),
        grid=(m // tt, n_chunks),
        in_specs=[pl.BlockSpec((tt, d), lambda i, e: (i, 0)),
                  pl.BlockSpec((ec, d), lambda i, e: (e, 0)),
                  pl.BlockSpec((d, ec), lambda i, e: (0, e)),
                  pl.BlockSpec((PEER_HEADS, PEER_N_KEYS, tt), lambda i, e: (0, 0, i)),
                  row_spec, row_spec,
                  pl.BlockSpec((tt, d), lambda i, e: (i, 0)),
                  pl.BlockSpec((1, 1, d), grp)],
        out_specs=pl.BlockSpec((tt, d), lambda i, e: (i, 0)),
        out_shape=jax.ShapeDtypeStruct((m, d), F32),
        scratch_shapes=[pltpu.VMEM((d, tt), F32), pltpu.VMEM((ec, tt), BF16), pltpu.VMEM((d, tt), BF16),
                        pltpu.VMEM((ec, tt), F32)],
        compiler_params=_cparams(("parallel", "arbitrary")),
        name="peerdense",
    )(h2, u, vt, e2, a1, e1, x, gate)


def _tiles(seq, ctx_len, batch):
    ctx_rows = batch * ctx_len
    return dict(
        mm=min(512, seq, ctx_rows),
        seqt=min(256, seq, ctx_len),
        peer=min(512, seq, ctx_rows),
        topk=min(256, seq, ctx_rows),
        scan=min(128, seq, ctx_len),
    )


def _rope_tables(batch, seq, ctx_len):
    inv = ROPE_THETA ** (-jnp.arange(0, ROPE_AXIS_DIM, 2, dtype=F32) / ROPE_AXIS_DIM)
    t = jnp.arange(seq, dtype=jnp.int32)
    ang_r = (t // GRID_W).astype(F32)[:, None] * inv[None, :]
    ang_c = (t % GRID_W).astype(F32)[:, None] * inv[None, :]
    cos = jnp.concatenate([jnp.cos(ang_r)] * 2 + [jnp.cos(ang_c)] * 2, axis=-1)
    sin = jnp.concatenate([-jnp.sin(ang_r), jnp.sin(ang_r), -jnp.sin(ang_c), jnp.sin(ang_c)], axis=-1)
    n_ctx = batch * ctx_len
    cos = jnp.concatenate([jnp.tile(cos, (batch, 1)), jnp.ones((n_ctx, HEAD_DIM), F32)], axis=0)
    sin = jnp.concatenate([jnp.tile(sin, (batch, 1)), jnp.zeros((n_ctx, HEAD_DIM), F32)], axis=0)
    return jnp.tile(cos, (1, 2)), jnp.tile(sin, (1, 2))


def _block_diag2(w):
    z = jnp.zeros_like(w[0])
    return jnp.concatenate([jnp.concatenate([w[0], z], axis=1), jnp.concatenate([z, w[1]], axis=1)], axis=0)


def kernel(x, c, ctx, c_ctx, norm1_g, norm2_g, w_ada, b_ada, w_in, w_out, q_norm_g, k_norm_g, attn_sink, rwkv_mu, rwkv_w0, rwkv_w_up, rwkv_a0, rwkv_a_up, rwkv_g_up, rwkv_k_k, rwkv_k_a, rwkv_r_k, rwkv_ln_g, rwkv_ln_b, conv_w, conv_b, conv_norm_g, peer_wq, peer_keys, peer_u, peer_v):
    batch, seq, d = x.shape
    ctx_len = ctx.shape[1]
    depth = w_in.shape[0]
    n_lat = batch * seq
    tl = _tiles(seq, ctx_len, batch)
    assert seq % ATT_BLOCK == 0 and ctx_len % ATT_BLOCK == 0 and n_lat % ctx_len == 0
    assert (seq + ctx_len) % tl["scan"] == 0

    xt = jnp.concatenate([x.reshape(n_lat, d), ctx.reshape(batch * ctx_len, d)], axis=0)
    cc = jnp.concatenate([c, c_ctx[None, :], jnp.zeros((8 - batch - 1, d), F32)], axis=0)
    cos2, sin2 = _rope_tables(batch, seq, ctx_len)
    row = lambda a: a.reshape(1, -1)

    for i in range(depth):
        mod = _ada(cc, w_ada[i], row(b_ada[i]))[:batch + 1].reshape(batch + 1, 6, 1, d)
        sh1, sc1, g1, sh2, sc2, g2 = (mod[:, k] for k in range(6))

        w_in_b = w_in[i].astype(BF16)
        p_att, h1 = _modmm(xt, row(norm1_g[i]), sc1, sh1, w_in_b[:, :ATT_COLS], n_lat, seq, tl["mm"], ATT_COLS)
        p_rw = _mm(h1, w_in_b[:, ATT_COLS:ATT_COLS + RWKV_COLS], tl["mm"], RWKV_COLS)
        p_cv = _mm(h1, w_in_b[:, ATT_COLS + RWKV_COLS:], tl["mm"], 2 * CONV_WIDTH)

        gain = jnp.concatenate([jnp.tile(q_norm_g[i] * HEAD_DIM ** -0.5, ATT_HEADS), jnp.tile(k_norm_g[i], ATT_KV_HEADS)])
        qk, vv = _qkprep(p_att, row(gain), cos2, sin2, tl["seqt"])
        y_att = _attention(qk, vv, attn_sink[i], batch, seq, ctx_len)

        jv, misc = _rwkvfeat(p_rw, row(rwkv_mu[i]), row(rwkv_w0[i]), _block_diag2(rwkv_w_up[i]),
                             row(rwkv_a0[i]), _block_diag2(rwkv_a_up[i]), rwkv_g_up[i],
                             row(rwkv_k_k[i]), row(rwkv_k_a[i]), tl["seqt"], n_lat, seq, ctx_len)
        y2 = _rwkv_scan(jv, misc, batch, seq, ctx_len, tl["scan"])
        y_rw = _rwkvout(y2, misc, row(rwkv_r_k[i]), row(rwkv_ln_g[i]), row(rwkv_ln_b[i]), tl["seqt"],
                        n_lat, seq, ctx_len)

        y_cv = _conv(p_cv, conv_w[i], row(conv_b[i]), row(conv_norm_g[i]), tl["seqt"], n_lat, seq, ctx_len)

        xt = _outproj(y_att, y_rw, y_cv, xt, g1, w_out[i].astype(BF16), n_lat, seq, tl["mm"], d)

        q, h2 = _modmm(xt, row(norm2_g[i]), sc2, sh2, peer_wq[i].astype(BF16), n_lat, seq, tl["mm"], peer_wq.shape[2])
        e2, a1, e1 = _peertopk(q, peer_keys[i], tl["topk"])
        xt = _peerdense(h2, peer_u[i].astype(BF16), peer_v[i].T.astype(BF16), e2, a1, e1, xt, g2,
                        n_lat, seq, tl["peer"], 1024, n_lat if i == depth - 1 else xt.shape[0])
    return xt.reshape(batch, seq, d)
```
